```python
import jax, jax.numpy as jnp
from jax import lax
import numpy as np

D_MODEL = 1024
BATCH = 4
SEQ = 4096
DEPTH = 4
DEC_BATCH = 8
DEC_SEQ = 2048
PAST_LEN = 128

POOL_WINDOWS = (2, 4, 8, 16)
N_POOL_GROUPS = 4
POOL_GROUP_DIM = 128
POOL_WIDTH = N_POOL_GROUPS * POOL_GROUP_DIM
MLA_HEADS = 8
QK_NOPE = 64
QK_ROPE = 32
QK_DIM = QK_NOPE + QK_ROPE
V_DIM = 64
Q_LORA = 384
KV_LORA = 256
MLA_WIDTH = MLA_HEADS * V_DIM
ROPE_THETA = 10000.0
Q_BLOCK = 128
MEM_TOKENS = 256
MEM_HEADS = 4
MEM_HEAD_DIM = 128
MEM_WIDTH = MEM_HEADS * MEM_HEAD_DIM
N_BRANCH = 3
IN_WIDTH = POOL_WIDTH + Q_LORA + KV_LORA + QK_ROPE + MEM_WIDTH + N_BRANCH * D_MODEL
D_FF = 2816
N_EXPERTS = 8
TOP_K = 2
N_DENSE = (DEPTH + 1) // 2
N_MOE = DEPTH // 2
EPS = 1e-6

kernel_name = "gated_hybrid_pool_mla_mem_encoder"


def rmsnorm(x, g):
    xf = x.astype(jnp.float32)
    y = xf * lax.rsqrt(jnp.mean(xf * xf, axis=-1, keepdims=True) + EPS)
    return y.astype(x.dtype) * g


def rope(x, pos):
    half = QK_ROPE // 2
    freqs = jnp.power(ROPE_THETA, -jnp.arange(half, dtype=jnp.float32) / half)
    ang = pos.astype(jnp.float32)[:, None] * freqs[None, :]
    c = jnp.cos(ang)[:, None, :].astype(x.dtype)
    s = jnp.sin(ang)[:, None, :].astype(x.dtype)
    x1, x2 = x[..., :half], x[..., half:]
    return jnp.concatenate([x1 * c - x2 * s, x1 * s + x2 * c], axis=-1)


def pool_mixer(u, w_grp, scale):
    B, S, _ = u.shape
    uf = u.astype(jnp.float32)
    c = jnp.pad(jnp.cumsum(uf, axis=1), ((0, 0), (1, 0), (0, 0)))
    t = jnp.arange(S)
    outs = []
    for gi, w in enumerate(POOL_WINDOWS):
        sl = slice(gi * POOL_GROUP_DIM, (gi + 1) * POOL_GROUP_DIM)
        lo = jnp.maximum(t - w // 2, 0)
        hi = jnp.minimum(t + w // 2, S)
        cg = c[:, :, sl]
        mean = (cg[:, hi] - cg[:, lo]) / (hi - lo).astype(jnp.float32)[None, :, None]
        outs.append(mean - uf[:, :, sl])
    d = jnp.stack(outs, axis=2).astype(u.dtype)
    y = jnp.einsum('bsgc,gcd->bsgd', d, w_grp).reshape(B, S, POOL_WIDTH)
    return y * scale


def mla(cq, ckv, kr, g_cq, w_uq, g_ckv, w_ukv, g_qn, g_kn, pos):
    B, S, _ = cq.shape
    q = (rmsnorm(cq, g_cq) @ w_uq).reshape(B, S, MLA_HEADS, QK_DIM)
    q = jnp.concatenate([q[..., :QK_NOPE], rope(q[..., QK_NOPE:], pos)], axis=-1)
    kv = (rmsnorm(ckv, g_ckv) @ w_ukv).reshape(B, S, MLA_HEADS, QK_NOPE + V_DIM)
    k_rope = rope(kr[:, :, None, :], pos)
    k = jnp.concatenate([kv[..., :QK_NOPE],
                         jnp.broadcast_to(k_rope, (B, S, MLA_HEADS, QK_ROPE))], axis=-1)
    v = kv[..., QK_NOPE:]
    q = rmsnorm(q, g_qn)
    k = rmsnorm(k, g_kn)
    scale = QK_DIM ** -0.5
    qb = q.reshape(B, S // Q_BLOCK, Q_BLOCK, MLA_HEADS, QK_DIM).swapaxes(0, 1)

    def attend(qi):
        s = jnp.einsum('bqhd,bkhd->bhqk', qi, k).astype(jnp.float32) * scale
        p = jax.nn.softmax(s, axis=-1).astype(v.dtype)
        return jnp.einsum('bhqk,bkhd->bqhd', p, v)

    o = lax.map(attend, qb)
    return o.swapaxes(0, 1).reshape(B, S, MLA_WIDTH)


def mem_attn(qm, mem, g_mem, w_mem_kv, g_qn, g_kn):
    B, S, _ = qm.shape
    M = mem.shape[1]
    q = rmsnorm(qm.reshape(B, S, MEM_HEADS, MEM_HEAD_DIM), g_qn)
    kv = rmsnorm(mem, g_mem) @ w_mem_kv
    k = rmsnorm(kv[..., :MEM_WIDTH].reshape(B, M, MEM_HEADS, MEM_HEAD_DIM), g_kn)
    v = kv[..., MEM_WIDTH:].reshape(B, M, MEM_HEADS, MEM_HEAD_DIM)
    s = jnp.einsum('bqhd,bkhd->bhqk', q, k).astype(jnp.float32) * (MEM_HEAD_DIM ** -0.5)
    p = jax.nn.softmax(s, axis=-1).astype(v.dtype)
    return jnp.einsum('bhqk,bkhd->bqhd', p, v).reshape(B, S, MEM_WIDTH)


def swiglu(h, w_gu, w_down):
    gu = h @ w_gu
    g, u = gu[..., :D_FF], gu[..., D_FF:]
    return (jax.nn.silu(g) * u) @ w_down


def moe(h, w_router, w_gu, w_down):
    logits = (h @ w_router).astype(jnp.float32)
    top_v, top_i = lax.top_k(logits, TOP_K)
    wts = jax.nn.softmax(top_v, axis=-1)
    combine = jnp.einsum('bsk,bske->bse', wts,
                         jax.nn.one_hot(top_i, N_EXPERTS, dtype=jnp.float32)).astype(h.dtype)
    out = jnp.zeros_like(h)
    for e in range(N_EXPERTS):
        out = out + combine[..., e:e + 1] * swiglu(h, w_gu[e], w_down[e])
    return out


def trunk(x, mem, g_mix, w_in, pool_w, pool_scale, g_cq, w_uq, g_ckv, w_ukv,
          g_qn_mla, g_kn_mla, g_mem, w_mem_kv, g_qn_mem, g_kn_mem, w_br, w_out,
          g_ffn, w_gu_dense, w_down_dense, w_router, w_gu_moe, w_down_moe):
    B, S, _ = x.shape
    pos = jnp.arange(S)
    cuts = [POOL_WIDTH]
    for wd in (Q_LORA, KV_LORA, QK_ROPE, MEM_WIDTH):
        cuts.append(cuts[-1] + wd)
    for l in range(DEPTH):
        h = rmsnorm(x, g_mix[l])
        z = h @ w_in[l]
        u_pool, cq, ckv, kr, qm, gl = jnp.split(z, cuts, axis=-1)
        y_pool = pool_mixer(u_pool, pool_w[l], pool_scale[l])
        y_mla = mla(cq, ckv, kr, g_cq[l], w_uq[l], g_ckv[l], w_ukv[l],
                    g_qn_mla[l], g_kn_mla[l], pos)
        y_mem = mem_attn(qm, mem, g_mem[l], w_mem_kv[l], g_qn_mem[l], g_kn_mem[l])
        gate = jax.nn.sigmoid(gl.astype(jnp.float32)).astype(x.dtype).reshape(B, S, N_BRANCH, D_MODEL)
        merged = (gate[:, :, 0] * (y_pool @ w_br[l, 0])
                  + gate[:, :, 1] * (y_mla @ w_br[l, 1])
                  + gate[:, :, 2] * (y_mem @ w_br[l, 2]))
        x = x + merged @ w_out[l]
        hf = rmsnorm(x, g_ffn[l])
        if l % 2 == 0:
            x = x + swiglu(hf, w_gu_dense[l // 2], w_down_dense[l // 2])
        else:
            x = x + moe(hf, w_router[l // 2], w_gu_moe[l // 2], w_down_moe[l // 2])
    return x


def setup_inputs(seed: int = 0) -> dict:
    key = jax.random.key(seed)
    ks = jax.random.split(key, 32)
    f32 = jnp.float32

    def nrm(k, shape, fan_in):
        return jax.random.normal(k, shape, f32) * (fan_in ** -0.5)

    def gain(k, shape):
        return 1.0 + 0.02 * jax.random.normal(k, shape, f32)

    return {
        "x_prompt": jax.random.normal(ks[0], (BATCH, SEQ, D_MODEL), f32),
        "x_sample": jax.random.normal(ks[1], (DEC_BATCH, DEC_SEQ, D_MODEL), f32),
        "mem_prompt": jax.random.normal(ks[2], (BATCH, MEM_TOKENS, D_MODEL), f32),
        "mem_sample": jax.random.normal(ks[3], (DEC_BATCH, MEM_TOKENS, D_MODEL), f32),
        "g_mix": gain(ks[4], (DEPTH, D_MODEL)),
        "w_in": nrm(ks[5], (DEPTH, D_MODEL, IN_WIDTH), D_MODEL),
        "pool_w": nrm(ks[6], (DEPTH, N_POOL_GROUPS, POOL_GROUP_DIM, POOL_GROUP_DIM), POOL_GROUP_DIM),
        "pool_scale": 1.0 + 0.1 * jax.random.normal(ks[7], (DEPTH, POOL_WIDTH), f32),
        "g_cq": gain(ks[8], (DEPTH, Q_LORA)),
        "w_uq": nrm(ks[9], (DEPTH, Q_LORA, MLA_HEADS * QK_DIM), Q_LORA),
        "g_ckv": gain(ks[10], (DEPTH, KV_LORA)),
        "w_ukv": nrm(ks[11], (DEPTH, KV_LORA, MLA_HEADS * (QK_NOPE + V_DIM)), KV_LORA),
        "g_qn_mla": gain(ks[12], (DEPTH, QK_DIM)),
        "g_kn_mla": gain(ks[13], (DEPTH, QK_DIM)),
        "g_mem": gain(ks[14], (DEPTH, D_MODEL)),
        "w_mem_kv": nrm(ks[15], (DEPTH, D_MODEL, 2 * MEM_WIDTH), D_MODEL),
        "g_qn_mem": gain(ks[16], (DEPTH, MEM_HEAD_DIM)),
        "g_kn_mem": gain(ks[17], (DEPTH, MEM_HEAD_DIM)),
        "w_br": nrm(ks[18], (DEPTH, N_BRANCH, POOL_WIDTH, D_MODEL), POOL_WIDTH),
        "w_out": nrm(ks[19], (DEPTH, D_MODEL, D_MODEL), D_MODEL),
        "g_ffn": gain(ks[20], (DEPTH, D_MODEL)),
        "w_gu_dense": nrm(ks[21], (N_DENSE, D_MODEL, 2 * D_FF), D_MODEL),
        "w_down_dense": nrm(ks[22], (N_DENSE, D_FF, D_MODEL), D_FF),
        "w_router": nrm(ks[23], (N_MOE, D_MODEL, N_EXPERTS), D_MODEL),
        "w_gu_moe": nrm(ks[24], (N_MOE, N_EXPERTS, D_MODEL, 2 * D_FF), D_MODEL),
        "w_down_moe": nrm(ks[25], (N_MOE, N_EXPERTS, D_FF, D_MODEL), D_FF),
    }


def reference(x_prompt, x_sample, mem_prompt, mem_sample, g_mix, w_in, pool_w, pool_scale,
              g_cq, w_uq, g_ckv, w_ukv, g_qn_mla, g_kn_mla, g_mem, w_mem_kv, g_qn_mem,
              g_kn_mem, w_br, w_out, g_ffn, w_gu_dense, w_down_dense, w_router,
              w_gu_moe, w_down_moe):
    y_prompt = trunk(x_prompt, mem_prompt, g_mix, w_in, pool_w, pool_scale, g_cq, w_uq,
                     g_ckv, w_ukv, g_qn_mla, g_kn_mla, g_mem, w_mem_kv, g_qn_mem, g_kn_mem,
                     w_br, w_out, g_ffn, w_gu_dense, w_down_dense, w_router, w_gu_moe,
                     w_down_moe)
    y_sample = trunk(x_sample, mem_sample, g_mix, w_in, pool_w, pool_scale, g_cq, w_uq,
                     g_ckv, w_ukv, g_qn_mla, g_kn_mla, g_mem, w_mem_kv, g_qn_mem, g_kn_mem,
                     w_br, w_out, g_ffn, w_gu_dense, w_down_dense, w_router, w_gu_moe,
                     w_down_moe)
    return (y_prompt, y_sample)
```

```python
import functools

import jax
import jax.numpy as jnp
from jax import lax
from jax.experimental import pallas as pl
from jax.experimental.pallas import tpu as pltpu

D_MODEL = 1024
DEPTH = 4
POOL_WINDOWS = (2, 4, 8, 16)
N_POOL_GROUPS = 4
POOL_GROUP_DIM = 128
POOL_WIDTH = 512
MLA_HEADS = 8
QK_NOPE = 64
QK_ROPE = 32
QK_DIM = 96
V_DIM = 64
Q_LORA = 384
KV_LORA = 256
ROPE_THETA = 10000.0
MEM_TOKENS = 256
MEM_HEADS = 4
MEM_HEAD_DIM = 128
MEM_WIDTH = 512
N_BRANCH = 3
D_FF = 2816
N_EXPERTS = 8
EPS = 1e-6

LANES = 128
POOL_HALO = 16
VMEM_LIMIT_BYTES = 56 * 1024 * 1024

C_POOL = 0
C_CQ = C_POOL + POOL_WIDTH
C_CKV = C_CQ + Q_LORA
C_KR = C_CKV + KV_LORA
C_QM = C_KR + LANES
C_GATE = C_QM + MEM_WIDTH
IN_PACKED = C_GATE + N_BRANCH * D_MODEL

BF16 = jnp.bfloat16
F32 = jnp.float32
NT_DIMS = (((1,), (1,)), ((), ()))


def _const_spec(shape):
    zeros = (0,) * len(shape)
    return pl.BlockSpec(shape, lambda *_: zeros, pipeline_mode=pl.Buffered(1))


def _params(*sem):
    return pltpu.CompilerParams(dimension_semantics=sem, vmem_limit_bytes=VMEM_LIMIT_BYTES)


def _rms(x, width):
    return x * lax.rsqrt(jnp.sum(x * x, axis=-1, keepdims=True) * (1.0 / width) + EPS)


def _dot(a, b):
    return jnp.dot(a, b, preferred_element_type=F32)


def _mix_in_kernel(x_ref, gmix_ref, win_ref, gcq_ref, wuq_ref, gckv_ref, wuk_ref, wuv_ref,
                   cos_ref, sina_ref, sinb_ref, gqn_ref, gkn_ref, gqm_ref,
                   up_ref, q_ref, k_ref, v_ref, qm_ref, gate_ref):
    hn = (_rms(x_ref[0], D_MODEL) * gmix_ref[...]).astype(BF16)

    def proj(lo, hi):
        return _dot(hn, win_ref[:, lo:hi])

    cos, sina, sinb = cos_ref[...], sina_ref[...], sinb_ref[...]

    def rope(t):
        return t * cos + pltpu.roll(t, LANES - QK_ROPE // 2, 1) * sina + pltpu.roll(t, QK_ROPE // 2, 1) * sinb

    up_ref[0] = proj(C_POOL, C_CQ).astype(BF16)

    cqn = (_rms(proj(C_CQ, C_CKV), Q_LORA) * gcq_ref[...]).astype(BF16)
    qf = _dot(cqn, wuq_ref[...])
    gq = gqn_ref[...] * (QK_DIM ** -0.5)
    for h in range(MLA_HEADS):
        qh = rope(qf[:, h * LANES:(h + 1) * LANES])
        q_ref[0, h] = (_rms(qh, QK_DIM) * gq).astype(BF16)

    ckvn = (_rms(proj(C_CKV, C_KR), KV_LORA) * gckv_ref[...]).astype(BF16)
    kf = _dot(ckvn, wuk_ref[...])
    kr = rope(proj(C_KR, C_QM))
    gk = gkn_ref[...]
    for h in range(MLA_HEADS):
        kh = kf[:, h * LANES:(h + 1) * LANES] + kr
        k_ref[0, h] = (_rms(kh, QK_DIM) * gk).astype(BF16)
    v_ref[0] = _dot(ckvn, wuv_ref[...]).astype(BF16)

    qm = proj(C_QM, C_GATE)
    gm = gqm_ref[...] * (MEM_HEAD_DIM ** -0.5)
    for h in range(MEM_HEADS):
        qh = qm[:, h * LANES:(h + 1) * LANES]
        qm_ref[0, h] = (_rms(qh, MEM_HEAD_DIM) * gm).astype(BF16)

    for c in range(N_BRANCH):
        gl = proj(C_GATE + c * D_MODEL, C_GATE + (c + 1) * D_MODEL)
        gate_ref[0, :, c * D_MODEL:(c + 1) * D_MODEL] = jax.nn.sigmoid(gl).astype(BF16)


def _mix_in(x, lw, tables, tm):
    B, S, _ = x.shape
    cos, sina, sinb = tables
    row = lambda w: pl.BlockSpec((1, tm, w), lambda b, i: (b, i, 0))
    heads = lambda n: pl.BlockSpec((1, n, tm, LANES), lambda b, i: (b, 0, i, 0))
    tab = pl.BlockSpec((tm, LANES), lambda b, i: (i, 0))
    return pl.pallas_call(
        _mix_in_kernel,
        grid=(B, S // tm),
        in_specs=[row(D_MODEL), _const_spec((1, D_MODEL)), _const_spec((D_MODEL, IN_PACKED)),
                  _const_spec((1, Q_LORA)), _const_spec((Q_LORA, MLA_HEADS * LANES)),
                  _const_spec((1, KV_LORA)), _const_spec((KV_LORA, MLA_HEADS * LANES)),
                  _const_spec((KV_LORA, MLA_HEADS * V_DIM)),
                  tab, tab, tab,
                  _const_spec((1, LANES)), _const_spec((1, LANES)), _const_spec((1, LANES))],
        out_specs=[row(POOL_WIDTH), heads(MLA_HEADS), heads(MLA_HEADS), row(MLA_HEADS * V_DIM),
                   heads(MEM_HEADS), row(N_BRANCH * D_MODEL)],
        out_shape=[jax.ShapeDtypeStruct((B, S, POOL_WIDTH), BF16),
                   jax.ShapeDtypeStruct((B, MLA_HEADS, S, LANES), BF16),
                   jax.ShapeDtypeStruct((B, MLA_HEADS, S, LANES), BF16),
                   jax.ShapeDtypeStruct((B, S, MLA_HEADS * V_DIM), BF16),
                   jax.ShapeDtypeStruct((B, MEM_HEADS, S, LANES), BF16),
                   jax.ShapeDtypeStruct((B, S, N_BRANCH * D_MODEL), BF16)],
        compiler_params=_params("parallel", "parallel"),
        name="mix_in",
    )(x, lw["g_mix"], lw["w_in"], lw["g_cq"], lw["w_uq"], lw["g_ckv"], lw["w_uk"], lw["w_uv"],
      cos, sina, sinb, lw["g_qn_mla"], lw["g_kn_mla"], lw["g_qn_mem"])


def _mem_kv_kernel(mem_ref, gmem_ref, w_ref, gkn_ref, km_ref, vm_ref):
    hn = (_rms(mem_ref[0], D_MODEL) * gmem_ref[...]).astype(BF16)
    kv = _dot(hn, w_ref[...])
    for h in range(MEM_HEADS):
        kh = kv[:, h * LANES:(h + 1) * LANES]
        km_ref[0, h] = (_rms(kh, MEM_HEAD_DIM) * gkn_ref[...]).astype(BF16)
        vm_ref[0, h] = kv[:, MEM_WIDTH + h * LANES:MEM_WIDTH + (h + 1) * LANES].astype(BF16)


def _mem_kv(mem, lw):
    B, M, _ = mem.shape
    out = pl.BlockSpec((1, MEM_HEADS, M, LANES), lambda b: (b, 0, 0, 0))
    shape = jax.ShapeDtypeStruct((B, MEM_HEADS, M, LANES), BF16)
    return pl.pallas_call(
        _mem_kv_kernel,
        grid=(B,),
        in_specs=[pl.BlockSpec((1, M, D_MODEL), lambda b: (b, 0, 0)), _const_spec((1, D_MODEL)),
                  _const_spec((D_MODEL, 2 * MEM_WIDTH)), _const_spec((1, LANES))],
        out_specs=[out, out],
        out_shape=[shape, shape],
        compiler_params=_params("parallel"),
        name="mem_kv",
    )(mem, lw["g_mem"], lw["w_mem_kv"], lw["g_kn_mem"])


def _pool_kernel(seq_len, tp, cur_ref, prev_ref, next_ref, pw_ref, ps_ref, out_ref):
    r0 = pl.program_id(1) * tp
    t = r0 + lax.broadcasted_iota(jnp.int32, (tp, 1), 0)

    def band(col0, ncols, lo, hi):
        s = col0 + lax.broadcasted_iota(jnp.int32, (tp, ncols), 1)
        return jnp.where(s >= lo, jnp.where(s < hi, 1.0, 0.0), 0.0).astype(BF16)

    for g, w in enumerate(POOL_WINDOWS):
        sl = slice(g * POOL_GROUP_DIM, (g + 1) * POOL_GROUP_DIM)
        lo = jnp.maximum(t - w // 2, 0)
        hi = jnp.minimum(t + w // 2, seq_len)
        u = cur_ref[0, :, sl]
        wsum = (_dot(band(r0, tp, lo, hi), u)
                + _dot(band(r0 - POOL_HALO, POOL_HALO, lo, hi), prev_ref[0, :, sl])
                + _dot(band(r0 + tp, POOL_HALO, lo, hi), next_ref[0, :, sl]))
        d = wsum / (hi - lo).astype(F32) - u.astype(F32)
        y = _dot(d.astype(BF16), pw_ref[g]) * ps_ref[:, sl]
        out_ref[0, :, sl] = y.astype(BF16)


def _pool(up, lw, tp):
    B, S, _ = up.shape
    per = tp // POOL_HALO
    last = S // POOL_HALO - 1
    return pl.pallas_call(
        functools.partial(_pool_kernel, S, tp),
        grid=(B, S // tp),
        in_specs=[pl.BlockSpec((1, tp, POOL_WIDTH), lambda b, i: (b, i, 0)),
                  pl.BlockSpec((1, POOL_HALO, POOL_WIDTH), lambda b, i: (b, jnp.maximum(i * per - 1, 0), 0)),
                  pl.BlockSpec((1, POOL_HALO, POOL_WIDTH), lambda b, i: (b, jnp.minimum((i + 1) * per, last), 0)),
                  _const_spec((N_POOL_GROUPS, POOL_GROUP_DIM, POOL_GROUP_DIM)),
                  _const_spec((1, POOL_WIDTH))],
        out_specs=pl.BlockSpec((1, tp, POOL_WIDTH), lambda b, i: (b, i, 0)),
        out_shape=jax.ShapeDtypeStruct((B, S, POOL_WIDTH), BF16),
        compiler_params=_params("parallel", "parallel"),
        name="pool",
    )(up, up, up, lw["pool_w"], lw["pool_scale"])


def _softmax_pv(s, v):
    m = jnp.max(s, axis=-1, keepdims=True)
    p = jnp.exp(s - m)
    l = jnp.sum(p, axis=-1, keepdims=True)
    return _dot(p.astype(BF16), v) / l


def _mla_attn_kernel(q_ref, k_ref, v_ref, o_ref):
    v = v_ref[0]
    outs = []
    for hh in range(2):
        s = lax.dot_general(q_ref[0, hh], k_ref[0, hh], NT_DIMS, preferred_element_type=F32)
        outs.append(_softmax_pv(s, v))
    lane = lax.broadcasted_iota(jnp.int32, outs[0].shape, 1)
    o_ref[0] = jnp.where(lane < V_DIM, outs[0], outs[1]).astype(BF16)


def _mla_attn(q, k, v, tq):
    B, H, S, _ = q.shape
    return pl.pallas_call(
        _mla_attn_kernel,
        grid=(B, H // 2, S // tq),
        in_specs=[pl.BlockSpec((1, 2, tq, LANES), lambda b, p, i: (b, p, i, 0)),
                  pl.BlockSpec((1, 2, S, LANES), lambda b, p, i: (b, p, 0, 0)),
                  pl.BlockSpec((1, S, 2 * V_DIM), lambda b, p, i: (b, 0, p))],
        out_specs=pl.BlockSpec((1, tq, 2 * V_DIM), lambda b, p, i: (b, i, p)),
        out_shape=jax.ShapeDtypeStruct((B, S, H * V_DIM), BF16),
        compiler_params=_params("parallel", "parallel", "arbitrary"),
        name="mla_attn",
    )(q, k, v)


def _merge_kernel(is_moe, yp_ref, ym_ref, qm_ref, km_ref, vm_ref, gate_ref, wbr_ref, wout_ref,
                  x_ref, gffn_ref, *rest):
    if is_moe:
        wr_ref, xo_ref, hf_ref, comb_ref = rest
    else:
        xo_ref, hf_ref = rest

    mem = []
    for h in range(MEM_HEADS):
        s = lax.dot_general(qm_ref[0, h], km_ref[0, h], NT_DIMS, preferred_element_type=F32)
        mem.append(_softmax_pv(s, vm_ref[0, h]).astype(BF16))
    y_mem = jnp.concatenate(mem, axis=-1)

    branches = (yp_ref[0], ym_ref[0], y_mem)
    merged = None
    for c, y in enumerate(branches):
        term = gate_ref[0, :, c * D_MODEL:(c + 1) * D_MODEL].astype(F32) * _dot(y, wbr_ref[c])
        merged = term if merged is None else merged + term
    x_new = x_ref[0] + _dot(merged.astype(BF16), wout_ref[...])
    xo_ref[0] = x_new
    hf = _rms(x_new, D_MODEL) * gffn_ref[...]
    hf_ref[0] = hf.astype(BF16)

    if is_moe:
        logits = jnp.dot(hf, wr_ref[...], preferred_element_type=F32, precision=lax.Precision.HIGHEST)
        lane = lax.broadcasted_iota(jnp.int32, logits.shape, 1)
        lg = jnp.where(lane < N_EXPERTS, logits, -jnp.inf)
        m1 = jnp.max(lg, axis=-1, keepdims=True)
        i1 = jnp.min(jnp.where(lg == m1, lane, LANES), axis=-1, keepdims=True)
        lg2 = jnp.where(lane == i1, -jnp.inf, lg)
        m2 = jnp.max(lg2, axis=-1, keepdims=True)
        i2 = jnp.min(jnp.where(lg2 == m2, lane, LANES), axis=-1, keepdims=True)
        e = jnp.exp(m2 - m1)
        w1 = 1.0 / (1.0 + e)
        comb_ref[0] = jnp.where(lane == i1, w1, jnp.where(lane == i2, e * w1, 0.0))


def _merge(is_moe, yp, ym, qm, km, vm, gate, x, lw, tm):
    B, S, _ = x.shape
    M = km.shape[2]
    row = lambda w: pl.BlockSpec((1, tm, w), lambda b, i: (b, i, 0))
    memkv = pl.BlockSpec((1, MEM_HEADS, M, LANES), lambda b, i: (b, 0, 0, 0))
    in_specs = [row(POOL_WIDTH), row(MLA_HEADS * V_DIM),
                pl.BlockSpec((1, MEM_HEADS, tm, LANES), lambda b, i: (b, 0, i, 0)), memkv, memkv,
                row(N_BRANCH * D_MODEL), _const_spec((N_BRANCH, POOL_WIDTH, D_MODEL)),
                _const_spec((D_MODEL, D_MODEL)), row(D_MODEL), _const_spec((1, D_MODEL))]
    args = [yp, ym, qm, km, vm, gate, lw["w_br"], lw["w_out"], x, lw["g_ffn"]]
    out_specs = [row(D_MODEL), row(D_MODEL)]
    out_shape = [jax.ShapeDtypeStruct((B, S, D_MODEL), F32), jax.ShapeDtypeStruct((B, S, D_MODEL), BF16)]
    if is_moe:
        in_specs.append(_const_spec((D_MODEL, LANES)))
        args.append(lw["w_router"])
        out_specs.append(row(LANES))
        out_shape.append(jax.ShapeDtypeStruct((B, S, LANES), F32))
    return pl.pallas_call(
        functools.partial(_merge_kernel, is_moe),
        grid=(B, S // tm),
        in_specs=in_specs, out_specs=out_specs, out_shape=out_shape,
        compiler_params=_params("parallel", "parallel"),
        name="merge_moe" if is_moe else "merge",
    )(*args)


def _swiglu_chunk(h, wg, wu):
    g = _dot(h, wg)
    return g * jax.nn.sigmoid(g) * _dot(h, wu)


def _ffn_kernel(nj, hf_ref, wg_ref, wu_ref, wd_ref, x_ref, o_ref, acc_ref):
    j = pl.program_id(1)

    @pl.when(j == 0)
    def _():
        acc_ref[...] = jnp.zeros_like(acc_ref)

    a = _swiglu_chunk(hf_ref[...], wg_ref[...], wu_ref[...])
    acc_ref[...] += _dot(a.astype(BF16), wd_ref[...])

    @pl.when(j == nj - 1)
    def _():
        o_ref[...] = x_ref[...] + acc_ref[...]


def _ffn(hf, x, w_gu, w_down, tm, tf):
    N = hf.shape[0]
    nj = D_FF // tf
    return pl.pallas_call(
        functools.partial(_ffn_kernel, nj),
        grid=(N // tm, nj),
        in_specs=[pl.BlockSpec((tm, D_MODEL), lambda i, j: (i, 0)),
                  pl.BlockSpec((D_MODEL, tf), lambda i, j: (0, j)),
                  pl.BlockSpec((D_MODEL, tf), lambda i, j: (0, nj + j)),
                  pl.BlockSpec((tf, D_MODEL), lambda i, j: (j, 0)),
                  pl.BlockSpec((tm, D_MODEL), lambda i, j: (i, 0))],
        out_specs=pl.BlockSpec((tm, D_MODEL), lambda i, j: (i, 0)),
        out_shape=jax.ShapeDtypeStruct((N, D_MODEL), F32),
        scratch_shapes=[pltpu.VMEM((tm, D_MODEL), F32)],
        compiler_params=_params("parallel", "arbitrary"),
        name="ffn",
    )(hf, w_gu, w_gu, w_down, x)


def _moe_kernel(nj, hf_ref, comb_ref, wg_ref, wu_ref, wd_ref, x_ref, o_ref, acc_ref):
    e, j = pl.program_id(1), pl.program_id(2)

    @pl.when((e == 0) & (j == 0))
    def _():
        acc_ref[...] = jnp.zeros_like(acc_ref)

    comb = comb_ref[...]
    lane = lax.broadcasted_iota(jnp.int32, comb.shape, 1)
    c = jnp.sum(jnp.where(lane == e, comb, 0.0), axis=-1, keepdims=True)
    a = c * _swiglu_chunk(hf_ref[...], wg_ref[0], wu_ref[0])
    acc_ref[...] += _dot(a.astype(BF16), wd_ref[0])

    @pl.when((e == N_EXPERTS - 1) & (j == nj - 1))
    def _():
        o_ref[...] = x_ref[...] + acc_ref[...]


def _moe(hf, comb, x, w_gu, w_down, tm, tf):
    N = hf.shape[0]
    nj = D_FF // tf
    return pl.pallas_call(
        functools.partial(_moe_kernel, nj),
        grid=(N // tm, N_EXPERTS, nj),
        in_specs=[pl.BlockSpec((tm, D_MODEL), lambda i, e, j: (i, 0)),
                  pl.BlockSpec((tm, LANES), lambda i, e, j: (i, 0)),
                  pl.BlockSpec((1, D_MODEL, tf), lambda i, e, j: (e, 0, j)),
                  pl.BlockSpec((1, D_MODEL, tf), lambda i, e, j: (e, 0, nj + j)),
                  pl.BlockSpec((1, tf, D_MODEL), lambda i, e, j: (e, j, 0)),
                  pl.BlockSpec((tm, D_MODEL), lambda i, e, j: (i, 0))],
        out_specs=pl.BlockSpec((tm, D_MODEL), lambda i, e, j: (i, 0)),
        out_shape=jax.ShapeDtypeStruct((N, D_MODEL), F32),
        scratch_shapes=[pltpu.VMEM((tm, D_MODEL), F32)],
        compiler_params=_params("parallel", "arbitrary", "arbitrary"),
        name="moe",
    )(hf, comb, w_gu, w_gu, w_down, x)


def _rope_tables(seq_len):
    half = QK_ROPE // 2
    freqs = jnp.power(ROPE_THETA, -jnp.arange(half, dtype=F32) / half)
    ang = jnp.arange(seq_len).astype(F32)[:, None] * freqs[None, :]
    c, s = jnp.cos(ang), jnp.sin(ang)
    z = lambda n: jnp.zeros((seq_len, n), F32)
    o = lambda n: jnp.ones((seq_len, n), F32)
    tail = LANES - QK_NOPE - QK_ROPE
    cos = jnp.concatenate([o(QK_NOPE), c, c, o(tail)], axis=-1)
    sina = jnp.concatenate([z(QK_NOPE), -s, z(half), z(tail)], axis=-1)
    sinb = jnp.concatenate([z(QK_NOPE), z(half), s, z(tail)], axis=-1)
    return cos, sina, sinb


def _pad_last(a, width):
    return jnp.pad(a, [(0, 0)] * (a.ndim - 1) + [(0, width - a.shape[-1])])


def _layer_weights(l, g_mix, w_in, pool_w, pool_scale, g_cq, w_uq, g_ckv, w_ukv, g_qn_mla, g_kn_mla,
                   g_mem, w_mem_kv, g_qn_mem, g_kn_mem, w_br, w_out, g_ffn, w_router):
    wi = w_in[l]
    kr_lo, kr_hi = POOL_WIDTH + Q_LORA + KV_LORA, POOL_WIDTH + Q_LORA + KV_LORA + QK_ROPE
    zc = lambda n: jnp.zeros((D_MODEL, n), F32)
    w_in_p = jnp.concatenate([wi[:, :kr_lo], zc(QK_NOPE), wi[:, kr_lo:kr_hi], zc(LANES - QK_NOPE - QK_ROPE),
                              wi[:, kr_hi:]], axis=-1)
    w_uq_p = _pad_last(w_uq[l].reshape(Q_LORA, MLA_HEADS, QK_DIM), LANES).reshape(Q_LORA, MLA_HEADS * LANES)
    ukv = w_ukv[l].reshape(KV_LORA, MLA_HEADS, QK_NOPE + V_DIM)
    w_uk_p = _pad_last(ukv[..., :QK_NOPE], LANES).reshape(KV_LORA, MLA_HEADS * LANES)
    w_uv_p = ukv[..., QK_NOPE:].reshape(KV_LORA, MLA_HEADS * V_DIM)
    row = lambda a: a.reshape(1, -1)
    lw = {
        "g_mix": row(g_mix[l]), "w_in": w_in_p.astype(BF16),
        "g_cq": row(g_cq[l]), "w_uq": w_uq_p.astype(BF16),
        "g_ckv": row(g_ckv[l]), "w_uk": w_uk_p.astype(BF16), "w_uv": w_uv_p.astype(BF16),
        "g_qn_mla": _pad_last(row(g_qn_mla[l]), LANES), "g_kn_mla": _pad_last(row(g_kn_mla[l]), LANES),
        "g_qn_mem": row(g_qn_mem[l]), "g_kn_mem": row(g_kn_mem[l]),
        "g_mem": row(g_mem[l]), "w_mem_kv": w_mem_kv[l].astype(BF16),
        "pool_w": pool_w[l].astype(BF16), "pool_scale": row(pool_scale[l]),
        "w_br": w_br[l].astype(BF16), "w_out": w_out[l].astype(BF16), "g_ffn": row(g_ffn[l]),
    }
    if l % 2 == 1:
        lw["w_router"] = _pad_last(w_router[l // 2], LANES)
    return lw


def _trunk(x, mem, layers, ffn_w):
    B, S, _ = x.shape
    tm = min(512, S)
    tq = min(256, S)
    tp = min(256, S)
    tn = min(1024, B * S)
    tables = _rope_tables(S)
    for l, lw in enumerate(layers):
        up, q, k, v, qm, gate = _mix_in(x, lw, tables, tm)
        km, vm = _mem_kv(mem, lw)
        yp = _pool(up, lw, tp)
        ym = _mla_attn(q, k, v, tq)
        w_gu, w_down = ffn_w[l]
        if l % 2 == 0:
            x_mid, hf = _merge(False, yp, ym, qm, km, vm, gate, x, lw, tm)
            x = _ffn(hf.reshape(B * S, D_MODEL), x_mid.reshape(B * S, D_MODEL), w_gu, w_down, tn, 256)
        else:
            x_mid, hf, comb = _merge(True, yp, ym, qm, km, vm, gate, x, lw, tm)
            x = _moe(hf.reshape(B * S, D_MODEL), comb.reshape(B * S, LANES), x_mid.reshape(B * S, D_MODEL),
                     w_gu, w_down, tn, 256)
        x = x.reshape(B, S, D_MODEL)
    return x


def kernel(x_prompt, x_sample, mem_prompt, mem_sample, g_mix, w_in, pool_w, pool_scale, g_cq, w_uq, g_ckv, w_ukv, g_qn_mla, g_kn_mla, g_mem, w_mem_kv, g_qn_mem, g_kn_mem, w_br, w_out, g_ffn, w_gu_dense, w_down_dense, w_router, w_gu_moe, w_down_moe):
    layers = [_layer_weights(l, g_mix, w_in, pool_w, pool_scale, g_cq, w_uq, g_ckv, w_ukv, g_qn_mla,
                             g_kn_mla, g_mem, w_mem_kv, g_qn_mem, g_kn_mem, w_br, w_out, g_ffn, w_router)
              for l in range(DEPTH)]
    ffn_w = []
    for l in range(DEPTH):
        if l % 2 == 0:
            ffn_w.append((w_gu_dense[l // 2].astype(BF16), w_down_dense[l // 2].astype(BF16)))
        else:
            ffn_w.append((w_gu_moe[l // 2].astype(BF16), w_down_moe[l // 2].astype(BF16)))
    y_prompt = _trunk(x_prompt, mem_prompt, layers, ffn_w)
    y_sample = _trunk(x_sample, mem_sample, layers, ffn_w)
    return (y_prompt, y_sample)
```

```python
import functools

import jax
import jax.numpy as jnp
from jax import lax
from jax.experimental import pallas as pl
from jax.experimental.pallas import tpu as pltpu
from jax.experimental.pallas import tpu_sc as plsc

D_MODEL = 1024
DEPTH = 4
POOL_WINDOWS = (2, 4, 8, 16)
N_POOL_GROUPS = 4
POOL_GROUP_DIM = 128
POOL_WIDTH = 512
MLA_HEADS = 8
QK_NOPE = 64
QK_ROPE = 32
QK_DIM = 96
V_DIM = 64
Q_LORA = 384
KV_LORA = 256
ROPE_THETA = 10000.0
MEM_TOKENS = 256
MEM_HEADS = 4
MEM_HEAD_DIM = 128
MEM_WIDTH = 512
N_BRANCH = 3
D_FF = 2816
N_EXPERTS = 8
EPS = 1e-6

LANES = 128
POOL_HALO = 16
VMEM_LIMIT_BYTES = 56 * 1024 * 1024
MOE_TILE = 512
SC_WINDOW = 128
SC_COLS = 256
SC_CHUNKS = D_MODEL // SC_COLS

C_POOL = 0
C_CQ = C_POOL + POOL_WIDTH
C_CKV = C_CQ + Q_LORA
C_KR = C_CKV + KV_LORA
C_QM = C_KR + LANES
C_GATE = C_QM + MEM_WIDTH
IN_PACKED = C_GATE + N_BRANCH * D_MODEL

BF16 = jnp.bfloat16
F32 = jnp.float32
NT_DIMS = (((1,), (1,)), ((), ()))


def _const_spec(shape):
    zeros = (0,) * len(shape)
    return pl.BlockSpec(shape, lambda *_: zeros, pipeline_mode=pl.Buffered(1))


def _params(*sem):
    return pltpu.CompilerParams(dimension_semantics=sem, vmem_limit_bytes=VMEM_LIMIT_BYTES)


def _rms(x, width):
    return x * lax.rsqrt(jnp.sum(x * x, axis=-1, keepdims=True) * (1.0 / width) + EPS)


def _dot(a, b):
    return jnp.dot(a, b, preferred_element_type=F32)


def _mix_in_kernel(x_ref, gmix_ref, win_ref, gcq_ref, wuq_ref, gckv_ref, wuk_ref, wuv_ref,
                   cos_ref, sina_ref, sinb_ref, gqn_ref, gkn_ref, gqm_ref,
                   up_ref, q_ref, k_ref, v_ref, qm_ref, gate_ref):
    hn = (_rms(x_ref[0], D_MODEL) * gmix_ref[...]).astype(BF16)

    def proj(lo, hi):
        return _dot(hn, win_ref[:, lo:hi])

    cos, sina, sinb = cos_ref[...], sina_ref[...], sinb_ref[...]

    def rope(t):
        return t * cos + pltpu.roll(t, LANES - QK_ROPE // 2, 1) * sina + pltpu.roll(t, QK_ROPE // 2, 1) * sinb

    up_ref[0] = proj(C_POOL, C_CQ).astype(BF16)

    cqn = (_rms(proj(C_CQ, C_CKV), Q_LORA) * gcq_ref[...]).astype(BF16)
    qf = _dot(cqn, wuq_ref[...])
    gq = gqn_ref[...] * (QK_DIM ** -0.5)
    for h in range(MLA_HEADS):
        qh = rope(qf[:, h * LANES:(h + 1) * LANES])
        q_ref[0, h] = (_rms(qh, QK_DIM) * gq).astype(BF16)

    ckvn = (_rms(proj(C_CKV, C_KR), KV_LORA) * gckv_ref[...]).astype(BF16)
    kf = _dot(ckvn, wuk_ref[...])
    kr = rope(proj(C_KR, C_QM))
    gk = gkn_ref[...]
    for h in range(MLA_HEADS):
        kh = kf[:, h * LANES:(h + 1) * LANES] + kr
        k_ref[0, h] = (_rms(kh, QK_DIM) * gk).astype(BF16)
    v_ref[0] = _dot(ckvn, wuv_ref[...]).astype(BF16)

    qm = proj(C_QM, C_GATE)
    gm = gqm_ref[...] * (MEM_HEAD_DIM ** -0.5)
    for h in range(MEM_HEADS):
        qh = qm[:, h * LANES:(h + 1) * LANES]
        qm_ref[0, h] = (_rms(qh, MEM_HEAD_DIM) * gm).astype(BF16)

    for c in range(N_BRANCH):
        gl = proj(C_GATE + c * D_MODEL, C_GATE + (c + 1) * D_MODEL)
        gate_ref[0, :, c * D_MODEL:(c + 1) * D_MODEL] = jax.nn.sigmoid(gl).astype(BF16)


def _mix_in(x, lw, tables, tm):
    B, S, _ = x.shape
    cos, sina, sinb = tables
    row = lambda w: pl.BlockSpec((1, tm, w), lambda b, i: (b, i, 0))
    heads = lambda n: pl.BlockSpec((1, n, tm, LANES), lambda b, i: (b, 0, i, 0))
    tab = pl.BlockSpec((tm, LANES), lambda b, i: (i, 0))
    return pl.pallas_call(
        _mix_in_kernel,
        grid=(B, S // tm),
        in_specs=[row(D_MODEL), _const_spec((1, D_MODEL)), _const_spec((D_MODEL, IN_PACKED)),
                  _const_spec((1, Q_LORA)), _const_spec((Q_LORA, MLA_HEADS * LANES)),
                  _const_spec((1, KV_LORA)), _const_spec((KV_LORA, MLA_HEADS * LANES)),
                  _const_spec((KV_LORA, MLA_HEADS * V_DIM)),
                  tab, tab, tab,
                  _const_spec((1, LANES)), _const_spec((1, LANES)), _const_spec((1, LANES))],
        out_specs=[row(POOL_WIDTH), heads(MLA_HEADS), heads(MLA_HEADS), row(MLA_HEADS * V_DIM),
                   heads(MEM_HEADS), row(N_BRANCH * D_MODEL)],
        out_shape=[jax.ShapeDtypeStruct((B, S, POOL_WIDTH), BF16),
                   jax.ShapeDtypeStruct((B, MLA_HEADS, S, LANES), BF16),
                   jax.ShapeDtypeStruct((B, MLA_HEADS, S, LANES), BF16),
                   jax.ShapeDtypeStruct((B, S, MLA_HEADS * V_DIM), BF16),
                   jax.ShapeDtypeStruct((B, MEM_HEADS, S, LANES), BF16),
                   jax.ShapeDtypeStruct((B, S, N_BRANCH * D_MODEL), BF16)],
        compiler_params=_params("parallel", "parallel"),
        name="mix_in",
    )(x, lw["g_mix"], lw["w_in"], lw["g_cq"], lw["w_uq"], lw["g_ckv"], lw["w_uk"], lw["w_uv"],
      cos, sina, sinb, lw["g_qn_mla"], lw["g_kn_mla"], lw["g_qn_mem"])


def _mem_kv_kernel(mem_ref, gmem_ref, w_ref, gkn_ref, km_ref, vm_ref):
    hn = (_rms(mem_ref[0], D_MODEL) * gmem_ref[...]).astype(BF16)
    kv = _dot(hn, w_ref[...])
    for h in range(MEM_HEADS):
        kh = kv[:, h * LANES:(h + 1) * LANES]
        km_ref[0, h] = (_rms(kh, MEM_HEAD_DIM) * gkn_ref[...]).astype(BF16)
        vm_ref[0, h] = kv[:, MEM_WIDTH + h * LANES:MEM_WIDTH + (h + 1) * LANES].astype(BF16)


def _mem_kv(mem, lw):
    B, M, _ = mem.shape
    out = pl.BlockSpec((1, MEM_HEADS, M, LANES), lambda b: (b, 0, 0, 0))
    shape = jax.ShapeDtypeStruct((B, MEM_HEADS, M, LANES), BF16)
    return pl.pallas_call(
        _mem_kv_kernel,
        grid=(B,),
        in_specs=[pl.BlockSpec((1, M, D_MODEL), lambda b: (b, 0, 0)), _const_spec((1, D_MODEL)),
                  _const_spec((D_MODEL, 2 * MEM_WIDTH)), _const_spec((1, LANES))],
        out_specs=[out, out],
        out_shape=[shape, shape],
        compiler_params=_params("parallel"),
        name="mem_kv",
    )(mem, lw["g_mem"], lw["w_mem_kv"], lw["g_kn_mem"])


def _pool_kernel(seq_len, tp, cur_ref, prev_ref, next_ref, pw_ref, ps_ref, out_ref):
    r0 = pl.program_id(1) * tp
    t = r0 + lax.broadcasted_iota(jnp.int32, (tp, 1), 0)

    def band(col0, ncols, lo, hi):
        s = col0 + lax.broadcasted_iota(jnp.int32, (tp, ncols), 1)
        return jnp.where(s >= lo, jnp.where(s < hi, 1.0, 0.0), 0.0).astype(BF16)

    for g, w in enumerate(POOL_WINDOWS):
        sl = slice(g * POOL_GROUP_DIM, (g + 1) * POOL_GROUP_DIM)
        lo = jnp.maximum(t - w // 2, 0)
        hi = jnp.minimum(t + w // 2, seq_len)
        u = cur_ref[0, :, sl]
        wsum = (_dot(band(r0, tp, lo, hi), u)
                + _dot(band(r0 - POOL_HALO, POOL_HALO, lo, hi), prev_ref[0, :, sl])
                + _dot(band(r0 + tp, POOL_HALO, lo, hi), next_ref[0, :, sl]))
        d = wsum / (hi - lo).astype(F32) - u.astype(F32)
        y = _dot(d.astype(BF16), pw_ref[g]) * ps_ref[:, sl]
        out_ref[0, :, sl] = y.astype(BF16)


def _pool(up, lw, tp):
    B, S, _ = up.shape
    per = tp // POOL_HALO
    last = S // POOL_HALO - 1
    return pl.pallas_call(
        functools.partial(_pool_kernel, S, tp),
        grid=(B, S // tp),
        in_specs=[pl.BlockSpec((1, tp, POOL_WIDTH), lambda b, i: (b, i, 0)),
                  pl.BlockSpec((1, POOL_HALO, POOL_WIDTH), lambda b, i: (b, jnp.maximum(i * per - 1, 0), 0)),
                  pl.BlockSpec((1, POOL_HALO, POOL_WIDTH), lambda b, i: (b, jnp.minimum((i + 1) * per, last), 0)),
                  _const_spec((N_POOL_GROUPS, POOL_GROUP_DIM, POOL_GROUP_DIM)),
                  _const_spec((1, POOL_WIDTH))],
        out_specs=pl.BlockSpec((1, tp, POOL_WIDTH), lambda b, i: (b, i, 0)),
        out_shape=jax.ShapeDtypeStruct((B, S, POOL_WIDTH), BF16),
        compiler_params=_params("parallel", "parallel"),
        name="pool",
    )(up, up, up, lw["pool_w"], lw["pool_scale"])


def _softmax_pv(s, v):
    m = jnp.max(s, axis=-1, keepdims=True)
    p = jnp.exp(s - m)
    l = jnp.sum(p, axis=-1, keepdims=True)
    return _dot(p.astype(BF16), v) / l


def _mla_attn_kernel(q_ref, k_ref, v_ref, o_ref):
    v = v_ref[0]
    outs = []
    for hh in range(2):
        s = lax.dot_general(q_ref[0, hh], k_ref[0, hh], NT_DIMS, preferred_element_type=F32)
        outs.append(_softmax_pv(s, v))
    lane = lax.broadcasted_iota(jnp.int32, outs[0].shape, 1)
    o_ref[0] = jnp.where(lane < V_DIM, outs[0], outs[1]).astype(BF16)


def _mla_attn(q, k, v, tq):
    B, H, S, _ = q.shape
    return pl.pallas_call(
        _mla_attn_kernel,
        grid=(B, H // 2, S // tq),
        in_specs=[pl.BlockSpec((1, 2, tq, LANES), lambda b, p, i: (b, p, i, 0)),
                  pl.BlockSpec((1, 2, S, LANES), lambda b, p, i: (b, p, 0, 0)),
                  pl.BlockSpec((1, S, 2 * V_DIM), lambda b, p, i: (b, 0, p))],
        out_specs=pl.BlockSpec((1, tq, 2 * V_DIM), lambda b, p, i: (b, i, p)),
        out_shape=jax.ShapeDtypeStruct((B, S, H * V_DIM), BF16),
        compiler_params=_params("parallel", "parallel", "arbitrary"),
        name="mla_attn",
    )(q, k, v)


def _merge_kernel(is_moe, yp_ref, ym_ref, qm_ref, km_ref, vm_ref, gate_ref, wbr_ref, wout_ref,
                  x_ref, gffn_ref, *rest):
    if is_moe:
        wrh_ref, wrl_ref, xo_ref, *hf_refs, ridx_ref, rw_ref = rest
    else:
        xo_ref, hf_ref = rest

    mem = []
    for h in range(MEM_HEADS):
        s = lax.dot_general(qm_ref[0, h], km_ref[0, h], NT_DIMS, preferred_element_type=F32)
        mem.append(_softmax_pv(s, vm_ref[0, h]).astype(BF16))
    y_mem = jnp.concatenate(mem, axis=-1)

    branches = (yp_ref[0], ym_ref[0], y_mem)
    merged = None
    for c, y in enumerate(branches):
        term = gate_ref[0, :, c * D_MODEL:(c + 1) * D_MODEL].astype(F32) * _dot(y, wbr_ref[c])
        merged = term if merged is None else merged + term
    x_new = x_ref[0] + _dot(merged.astype(BF16), wout_ref[...])
    xo_ref[0] = x_new
    hf = _rms(x_new, D_MODEL) * gffn_ref[...]
    if not is_moe:
        hf_ref[0] = hf.astype(BF16)
    else:
        for c, ref in enumerate(hf_refs):
            ref[0] = hf[:, c * SC_COLS:(c + 1) * SC_COLS]
        hf_hi = hf.astype(BF16)
        hf_lo = (hf - hf_hi.astype(F32)).astype(BF16)
        logits = _dot(hf_hi, wrh_ref[...]) + _dot(hf_lo, wrh_ref[...]) + _dot(hf_hi, wrl_ref[...])
        lane = lax.broadcasted_iota(jnp.int32, logits.shape, 1)
        lg = jnp.where(lane < N_EXPERTS, logits, -jnp.inf)
        m1 = jnp.max(lg, axis=-1, keepdims=True)
        i1 = jnp.min(jnp.where(lg == m1, lane, LANES), axis=-1, keepdims=True)
        lg2 = jnp.where(lane == i1, -jnp.inf, lg)
        m2 = jnp.max(lg2, axis=-1, keepdims=True)
        i2 = jnp.min(jnp.where(lg2 == m2, lane, LANES), axis=-1, keepdims=True)
        e = jnp.exp(m2 - m1)
        w1 = 1.0 / (1.0 + e)
        ridx_ref[0] = jnp.where(lane == 0, i1, jnp.where(lane == 1, i2, 0))
        rw_ref[0] = jnp.where(lane == 0, w1, jnp.where(lane == 1, e * w1, 0.0))


def _merge(is_moe, yp, ym, qm, km, vm, gate, x, lw, tm):
    B, S, _ = x.shape
    M = km.shape[2]
    row = lambda w: pl.BlockSpec((1, tm, w), lambda b, i: (b, i, 0))
    memkv = pl.BlockSpec((1, MEM_HEADS, M, LANES), lambda b, i: (b, 0, 0, 0))
    in_specs = [row(POOL_WIDTH), row(MLA_HEADS * V_DIM),
                pl.BlockSpec((1, MEM_HEADS, tm, LANES), lambda b, i: (b, 0, i, 0)), memkv, memkv,
                row(N_BRANCH * D_MODEL), _const_spec((N_BRANCH, POOL_WIDTH, D_MODEL)),
                _const_spec((D_MODEL, D_MODEL)), row(D_MODEL), _const_spec((1, D_MODEL))]
    args = [yp, ym, qm, km, vm, gate, lw["w_br"], lw["w_out"], x, lw["g_ffn"]]
    out_specs = [row(D_MODEL)]
    out_shape = [jax.ShapeDtypeStruct((B, S, D_MODEL), F32)]
    if is_moe:
        in_specs += [_const_spec((D_MODEL, LANES)), _const_spec((D_MODEL, LANES))]
        args += [lw["w_router_hi"], lw["w_router_lo"]]
        out_specs += [row(SC_COLS)] * SC_CHUNKS + [row(LANES), row(LANES)]
        out_shape += ([jax.ShapeDtypeStruct((B, S, SC_COLS), F32)] * SC_CHUNKS
                      + [jax.ShapeDtypeStruct((B, S, LANES), jnp.int32), jax.ShapeDtypeStruct((B, S, LANES), F32)])
    else:
        out_specs.append(row(D_MODEL))
        out_shape.append(jax.ShapeDtypeStruct((B, S, D_MODEL), BF16))
    return pl.pallas_call(
        functools.partial(_merge_kernel, is_moe),
        grid=(B, S // tm),
        in_specs=in_specs, out_specs=out_specs, out_shape=out_shape,
        compiler_params=_params("parallel", "parallel"),
        name="merge_moe" if is_moe else "merge",
    )(*args)


def _swiglu_chunk(h, wg, wu):
    g = _dot(h, wg)
    return g * jax.nn.sigmoid(g) * _dot(h, wu)


def _ffn_kernel(nj, hf_ref, wg_ref, wu_ref, wd_ref, x_ref, o_ref, acc_ref):
    j = pl.program_id(1)

    @pl.when(j == 0)
    def _():
        acc_ref[...] = jnp.zeros_like(acc_ref)

    a = _swiglu_chunk(hf_ref[...], wg_ref[...], wu_ref[...])
    acc_ref[...] += _dot(a.astype(BF16), wd_ref[...])

    @pl.when(j == nj - 1)
    def _():
        o_ref[...] = x_ref[...] + acc_ref[...]


def _ffn(hf, x, w_gu, w_down, tm, tf):
    N = hf.shape[0]
    nj = D_FF // tf
    return pl.pallas_call(
        functools.partial(_ffn_kernel, nj),
        grid=(N // tm, nj),
        in_specs=[pl.BlockSpec((tm, D_MODEL), lambda i, j: (i, 0)),
                  pl.BlockSpec((D_MODEL, tf), lambda i, j: (0, j)),
                  pl.BlockSpec((D_MODEL, tf), lambda i, j: (0, nj + j)),
                  pl.BlockSpec((tf, D_MODEL), lambda i, j: (j, 0)),
                  pl.BlockSpec((tm, D_MODEL), lambda i, j: (i, 0))],
        out_specs=pl.BlockSpec((tm, D_MODEL), lambda i, j: (i, 0)),
        out_shape=jax.ShapeDtypeStruct((N, D_MODEL), F32),
        scratch_shapes=[pltpu.VMEM((tm, D_MODEL), F32)],
        compiler_params=_params("parallel", "arbitrary"),
        name="ffn",
    )(hf, w_gu, w_gu, w_down, x)


def _route_plan(ridx, tmx):
    n = ridx.shape[0]
    e = ridx[:, :2].T.reshape(2 * n)
    onehot = (e[:, None] == jnp.arange(N_EXPERTS, dtype=jnp.int32)[None, :]).astype(jnp.int32)
    csum = jnp.cumsum(onehot, axis=0)
    rank = jnp.sum((csum - onehot) * onehot, axis=-1)
    counts = csum[-1]
    padded = ((counts + tmx - 1) // tmx) * tmx
    ends = jnp.cumsum(padded)
    starts = ends - padded
    dest = jnp.sum(onehot * starts[None, :], axis=-1) + rank
    n_tiles = (2 * n) // tmx + N_EXPERTS
    tile_row0 = jnp.arange(n_tiles, dtype=jnp.int32) * tmx
    tile_e = jnp.minimum(jnp.sum((tile_row0[:, None] >= ends[None, :]).astype(jnp.int32), axis=-1),
                         N_EXPERTS - 1)
    group_end = jnp.take(starts + counts, tile_e)
    tile_valid = jnp.where(tile_row0 < ends[-1], jnp.clip(group_end - tile_row0, 0, tmx), 0)
    return dest.reshape(2, n).astype(jnp.int32), tile_e.astype(jnp.int32), tile_valid.astype(jnp.int32), n_tiles * tmx


def _sc_mesh():
    return plsc.VectorSubcoreMesh(core_axis_name="core", subcore_axis_name="subcore")


def _sc_scatter_rows(chunks, dest, n_out):
    n, width = chunks[0].shape
    per_choice = n // SC_WINDOW
    out_type = [jax.ShapeDtypeStruct((n_out, width), chunks[0].dtype)] * len(chunks)

    @pl.kernel(out_type=out_type, mesh=_sc_mesh(), scratch_types=[])
    def scatter(*refs):
        x_refs, i_hbm, o_refs = refs[:len(chunks)], refs[len(chunks)], refs[len(chunks) + 1:]
        for x_hbm, o_hbm in zip(x_refs, o_refs):
            def body(x_vmem, i_vmem, o_hbm=o_hbm):
                pltpu.sync_copy(x_vmem, o_hbm.at[i_vmem.at[0]])

            pltpu.emit_pipeline(
                body,
                grid=(2 * per_choice,),
                in_specs=[pl.BlockSpec((SC_WINDOW, width), lambda i: (i % per_choice, 0)),
                          pl.BlockSpec((1, SC_WINDOW), lambda i: (0, i))],
                out_specs=[],
                core_axis_name=("core", "subcore"),
                dimension_semantics=(pltpu.PARALLEL,),
            )(x_hbm, i_hbm)

    return scatter(*chunks, dest.reshape(1, 2 * n))


def _sc_gather_rows(chunks, idx):
    n = idx.shape[0]
    width = chunks[0].shape[1]
    out_type = [jax.ShapeDtypeStruct((n, width), chunks[0].dtype)] * len(chunks)

    @pl.kernel(out_type=out_type, mesh=_sc_mesh(), scratch_types=[])
    def gather(*refs):
        x_refs, i_hbm, o_refs = refs[:len(chunks)], refs[len(chunks)], refs[len(chunks) + 1:]
        for x_hbm, o_hbm in zip(x_refs, o_refs):
            def body(i_vmem, o_vmem, x_hbm=x_hbm):
                pltpu.sync_copy(x_hbm.at[i_vmem.at[0]], o_vmem)

            pltpu.emit_pipeline(
                body,
                grid=(n // SC_WINDOW,),
                in_specs=[pl.BlockSpec((1, SC_WINDOW), lambda i: (0, i))],
                out_specs=[pl.BlockSpec((SC_WINDOW, width), lambda i: (i, 0))],
                core_axis_name=("core", "subcore"),
                dimension_semantics=(pltpu.PARALLEL,),
            )(i_hbm, o_hbm)

    return gather(*chunks, idx.reshape(1, n))


def _moe_group_kernel(nj, te_ref, tv_ref, *refs):
    xs_refs, (wg_ref, wu_ref, wd_ref) = refs[:SC_CHUNKS], refs[SC_CHUNKS:SC_CHUNKS + 3]
    ys_refs, (h_ref, acc_ref) = refs[SC_CHUNKS + 3:2 * SC_CHUNKS + 3], refs[2 * SC_CHUNKS + 3:]
    t, j = pl.program_id(0), pl.program_id(1)
    valid = tv_ref[t]

    @pl.when(valid > 0)
    def _():
        @pl.when(j == 0)
        def _():
            row = lax.broadcasted_iota(jnp.int32, xs_refs[0].shape, 0)
            for c, xs_ref in enumerate(xs_refs):
                h_ref[:, c * SC_COLS:(c + 1) * SC_COLS] = jnp.where(row < valid, xs_ref[...], 0.0).astype(BF16)
            acc_ref[...] = jnp.zeros_like(acc_ref)

        a = _swiglu_chunk(h_ref[...], wg_ref[0], wu_ref[0])
        acc_ref[...] += _dot(a.astype(BF16), wd_ref[0])

        @pl.when(j == nj - 1)
        def _():
            for c, ys_ref in enumerate(ys_refs):
                ys_ref[...] = acc_ref[:, c * SC_COLS:(c + 1) * SC_COLS]


def _moe_group(xs, tile_e, tile_valid, w_gu, w_down, tmx, tf):
    n_rows = xs[0].shape[0]
    nj = D_FF // tf
    chunk = pl.BlockSpec((tmx, SC_COLS), lambda t, j, te, tv: (t, 0))
    grid_spec = pltpu.PrefetchScalarGridSpec(
        num_scalar_prefetch=2,
        grid=(n_rows // tmx, nj),
        in_specs=[chunk] * SC_CHUNKS + [
            pl.BlockSpec((1, D_MODEL, tf), lambda t, j, te, tv: (te[t], 0, j)),
            pl.BlockSpec((1, D_MODEL, tf), lambda t, j, te, tv: (te[t], 0, nj + j)),
            pl.BlockSpec((1, tf, D_MODEL), lambda t, j, te, tv: (te[t], j, 0))],
        out_specs=[chunk] * SC_CHUNKS,
        scratch_shapes=[pltpu.VMEM((tmx, D_MODEL), BF16), pltpu.VMEM((tmx, D_MODEL), F32)],
    )
    return pl.pallas_call(
        functools.partial(_moe_group_kernel, nj),
        grid_spec=grid_spec,
        out_shape=[jax.ShapeDtypeStruct((n_rows, SC_COLS), F32)] * SC_CHUNKS,
        compiler_params=_params("parallel", "arbitrary"),
        name="moe_group",
    )(tile_e, tile_valid, *xs, w_gu, w_gu, w_down)


def _moe_combine_kernel(x_ref, rw_ref, *refs):
    y_refs, o_ref = refs[:-1], refs[-1]
    rw = rw_ref[...]
    w0, w1 = rw[:, 0:1], rw[:, 1:2]
    for c in range(SC_CHUNKS):
        sl = slice(c * SC_COLS, (c + 1) * SC_COLS)
        o_ref[:, sl] = x_ref[:, sl] + w0 * y_refs[2 * c][...] + w1 * y_refs[2 * c + 1][...]


def _moe_combine(x, yg, rw, tm):
    n = x.shape[0]
    per_choice = n // tm
    row = pl.BlockSpec((tm, D_MODEL), lambda i: (i, 0))
    first = pl.BlockSpec((tm, SC_COLS), lambda i: (i, 0))
    second = pl.BlockSpec((tm, SC_COLS), lambda i: (i + per_choice, 0))
    args = []
    for y in yg:
        args += [y, y]
    return pl.pallas_call(
        _moe_combine_kernel,
        grid=(n // tm,),
        in_specs=[row, pl.BlockSpec((tm, LANES), lambda i: (i, 0))] + [first, second] * SC_CHUNKS,
        out_specs=row,
        out_shape=jax.ShapeDtypeStruct((n, D_MODEL), F32),
        compiler_params=_params("parallel"),
        name="moe_combine",
    )(x, rw, *args)


def _moe(hf_chunks, ridx, rw, x, w_gu, w_down, tm):
    n = x.shape[0]
    dest, tile_e, tile_valid, n_rows = _route_plan(ridx, MOE_TILE)
    xs = _sc_scatter_rows(hf_chunks, dest, n_rows)
    ys = _moe_group(xs, tile_e, tile_valid, w_gu, w_down, MOE_TILE, 256)
    yg = _sc_gather_rows(ys, dest.reshape(2 * n))
    return _moe_combine(x, yg, rw, tm)


def _rope_tables(seq_len):
    half = QK_ROPE // 2
    freqs = jnp.power(ROPE_THETA, -jnp.arange(half, dtype=F32) / half)
    ang = jnp.arange(seq_len).astype(F32)[:, None] * freqs[None, :]
    c, s = jnp.cos(ang), jnp.sin(ang)
    z = lambda n: jnp.zeros((seq_len, n), F32)
    o = lambda n: jnp.ones((seq_len, n), F32)
    tail = LANES - QK_NOPE - QK_ROPE
    cos = jnp.concatenate([o(QK_NOPE), c, c, o(tail)], axis=-1)
    sina = jnp.concatenate([z(QK_NOPE), -s, z(half), z(tail)], axis=-1)
    sinb = jnp.concatenate([z(QK_NOPE), z(half), s, z(tail)], axis=-1)
    return cos, sina, sinb


def _pad_last(a, width):
    return jnp.pad(a, [(0, 0)] * (a.ndim - 1) + [(0, width - a.shape[-1])])


def _layer_weights(l, g_mix, w_in, pool_w, pool_scale, g_cq, w_uq, g_ckv, w_ukv, g_qn_mla, g_kn_mla,
                   g_mem, w_mem_kv, g_qn_mem, g_kn_mem, w_br, w_out, g_ffn, w_router):
    wi = w_in[l]
    kr_lo, kr_hi = POOL_WIDTH + Q_LORA + KV_LORA, POOL_WIDTH + Q_LORA + KV_LORA + QK_ROPE
    zc = lambda n: jnp.zeros((D_MODEL, n), F32)
    w_in_p = jnp.concatenate([wi[:, :kr_lo], zc(QK_NOPE), wi[:, kr_lo:kr_hi], zc(LANES - QK_NOPE - QK_ROPE),
                              wi[:, kr_hi:]], axis=-1)
    w_uq_p = _pad_last(w_uq[l].reshape(Q_LORA, MLA_HEADS, QK_DIM), LANES).reshape(Q_LORA, MLA_HEADS * LANES)
    ukv = w_ukv[l].reshape(KV_LORA, MLA_HEADS, QK_NOPE + V_DIM)
    w_uk_p = _pad_last(ukv[..., :QK_NOPE], LANES).reshape(KV_LORA, MLA_HEADS * LANES)
    w_uv_p = ukv[..., QK_NOPE:].reshape(KV_LORA, MLA_HEADS * V_DIM)
    row = lambda a: a.reshape(1, -1)
    lw = {
        "g_mix": row(g_mix[l]), "w_in": w_in_p.astype(BF16),
        "g_cq": row(g_cq[l]), "w_uq": w_uq_p.astype(BF16),
        "g_ckv": row(g_ckv[l]), "w_uk": w_uk_p.astype(BF16), "w_uv": w_uv_p.astype(BF16),
        "g_qn_mla": _pad_last(row(g_qn_mla[l]), LANES), "g_kn_mla": _pad_last(row(g_kn_mla[l]), LANES),
        "g_qn_mem": row(g_qn_mem[l]), "g_kn_mem": row(g_kn_mem[l]),
        "g_mem": row(g_mem[l]), "w_mem_kv": w_mem_kv[l].astype(BF16),
        "pool_w": pool_w[l].astype(BF16), "pool_scale": row(pool_scale[l]),
        "w_br": w_br[l].astype(BF16), "w_out": w_out[l].astype(BF16), "g_ffn": row(g_ffn[l]),
    }
    if l % 2 == 1:
        wr = _pad_last(w_router[l // 2], LANES)
        lw["w_router_hi"] = wr.astype(BF16)
        lw["w_router_lo"] = (wr - lw["w_router_hi"].astype(F32)).astype(BF16)
    return lw


def _trunk(x, mem, layers, ffn_w):
    B, S, _ = x.shape
    tm = min(512, S)
    tq = min(256, S)
    tp = min(256, S)
    tn = min(1024, B * S)
    tables = _rope_tables(S)
    for l, lw in enumerate(layers):
        up, q, k, v, qm, gate = _mix_in(x, lw, tables, tm)
        km, vm = _mem_kv(mem, lw)
        yp = _pool(up, lw, tp)
        ym = _mla_attn(q, k, v, tq)
        w_gu, w_down = ffn_w[l]
        if l % 2 == 0:
            x_mid, hf = _merge(False, yp, ym, qm, km, vm, gate, x, lw, tm)
            x = _ffn(hf.reshape(B * S, D_MODEL), x_mid.reshape(B * S, D_MODEL), w_gu, w_down, tn, 256)
        else:
            x_mid, *hf, ridx, rw = _merge(True, yp, ym, qm, km, vm, gate, x, lw, tm)
            x = _moe([h.reshape(B * S, SC_COLS) for h in hf], ridx.reshape(B * S, LANES),
                     rw.reshape(B * S, LANES), x_mid.reshape(B * S, D_MODEL), w_gu, w_down, tn)
        x = x.reshape(B, S, D_MODEL)
    return x


def kernel(x_prompt, x_sample, mem_prompt, mem_sample, g_mix, w_in, pool_w, pool_scale, g_cq, w_uq, g_ckv, w_ukv, g_qn_mla, g_kn_mla, g_mem, w_mem_kv, g_qn_mem, g_kn_mem, w_br, w_out, g_ffn, w_gu_dense, w_down_dense, w_router, w_gu_moe, w_down_moe):
    layers = [_layer_weights(l, g_mix, w_in, pool_w, pool_scale, g_cq, w_uq, g_ckv, w_ukv, g_qn_mla,
                             g_kn_mla, g_mem, w_mem_kv, g_qn_mem, g_kn_mem, w_br, w_out, g_ffn, w_router)
              for l in range(DEPTH)]
    ffn_w = []
    for l in range(DEPTH):
        if l % 2 == 0:
            ffn_w.append((w_gu_dense[l // 2].astype(BF16), w_down_dense[l // 2].astype(BF16)))
        else:
            ffn_w.append((w_gu_moe[l // 2].astype(BF16), w_down_moe[l // 2].astype(BF16)))
    y_prompt = _trunk(x_prompt, mem_prompt, layers, ffn_w)
    y_sample = _trunk(x_sample, mem_sample, layers, ffn_w)
    return (y_prompt, y_sample)
```

```python
import functools

import jax
import jax.numpy as jnp
from jax import lax
from jax.experimental import pallas as pl
from jax.experimental.pallas import tpu as pltpu
from jax.experimental.pallas import tpu_sc as plsc

D_MODEL = 1024
DEPTH = 4
POOL_WINDOWS = (2, 4, 8, 16)
N_POOL_GROUPS = 4
POOL_GROUP_DIM = 128
POOL_WIDTH = 512
MLA_HEADS = 8
QK_NOPE = 64
QK_ROPE = 32
QK_DIM = 96
V_DIM = 64
Q_LORA = 384
KV_LORA = 256
ROPE_THETA = 10000.0
MEM_TOKENS = 256
MEM_HEADS = 4
MEM_HEAD_DIM = 128
MEM_WIDTH = 512
N_BRANCH = 3
D_FF = 2816
N_EXPERTS = 8
EPS = 1e-6
LOG2_E = 1.4426950408889634

LANES = 128
POOL_HALO = 16
VMEM_LIMIT_BYTES = 56 * 1024 * 1024
MOE_TILE = 512
FF_CHUNK = 256
SC_WINDOW = 128
SC_COLS = 256
SC_CHUNKS = D_MODEL // SC_COLS

C_POOL = 0
C_CQ = C_POOL + POOL_WIDTH
C_CKV = C_CQ + Q_LORA
C_KR = C_CKV + KV_LORA
C_QM = C_KR + LANES
C_GATE = C_QM + MEM_WIDTH
IN_PACKED = C_GATE + N_BRANCH * D_MODEL

BF16 = jnp.bfloat16
F32 = jnp.float32
NT_DIMS = (((1,), (1,)), ((), ()))


def _const_spec(shape):
    zeros = (0,) * len(shape)
    return pl.BlockSpec(shape, lambda *_: zeros, pipeline_mode=pl.Buffered(1))


def _params(*sem):
    return pltpu.CompilerParams(dimension_semantics=sem, vmem_limit_bytes=VMEM_LIMIT_BYTES)


def _rms(x, width):
    return x * lax.rsqrt(jnp.sum(x * x, axis=-1, keepdims=True) * (1.0 / width) + EPS)


def _dot(a, b):
    return jnp.dot(a, b, preferred_element_type=F32)


def _mix_in_kernel(x_ref, gmix_ref, win_ref, gcq_ref, wuq_ref, gckv_ref, wuk_ref, wuv_ref,
                   cos_ref, sina_ref, sinb_ref, gqn_ref, gkn_ref, gqm_ref,
                   up_ref, q_ref, k_ref, v_ref, qm_ref, gate_ref):
    hn = (_rms(x_ref[0], D_MODEL) * gmix_ref[...]).astype(BF16)

    def proj(lo, hi):
        return _dot(hn, win_ref[:, lo:hi])

    cos, sina, sinb = cos_ref[...], sina_ref[...], sinb_ref[...]

    def rope(t):
        return t * cos + pltpu.roll(t, LANES - QK_ROPE // 2, 1) * sina + pltpu.roll(t, QK_ROPE // 2, 1) * sinb

    up_ref[0] = proj(C_POOL, C_CQ).astype(BF16)

    cqn = (_rms(proj(C_CQ, C_CKV), Q_LORA) * gcq_ref[...]).astype(BF16)
    qf = _dot(cqn, wuq_ref[...])
    gq = gqn_ref[...] * (QK_DIM ** -0.5 * LOG2_E)
    for h in range(MLA_HEADS):
        qh = rope(qf[:, h * LANES:(h + 1) * LANES])
        q_ref[0, h] = (_rms(qh, QK_DIM) * gq).astype(BF16)

    ckvn = (_rms(proj(C_CKV, C_KR), KV_LORA) * gckv_ref[...]).astype(BF16)
    kf = _dot(ckvn, wuk_ref[...])
    kr = rope(proj(C_KR, C_QM))
    gk = gkn_ref[...]
    for h in range(MLA_HEADS):
        kh = kf[:, h * LANES:(h + 1) * LANES] + kr
        k_ref[0, h] = (_rms(kh, QK_DIM) * gk).astype(BF16)
    vf = _dot(ckvn, wuv_ref[...])
    lane = lax.broadcasted_iota(jnp.int32, (vf.shape[0], LANES), 1)
    for h in range(MLA_HEADS):
        v_ref[0, h] = jnp.where(lane < V_DIM, vf[:, h * LANES:(h + 1) * LANES], 1.0).astype(BF16)

    qm = proj(C_QM, C_GATE)
    gm = gqm_ref[...] * (MEM_HEAD_DIM ** -0.5)
    for h in range(MEM_HEADS):
        qh = qm[:, h * LANES:(h + 1) * LANES]
        qm_ref[0, h] = (_rms(qh, MEM_HEAD_DIM) * gm).astype(BF16)

    for c in range(N_BRANCH):
        gl = proj(C_GATE + c * D_MODEL, C_GATE + (c + 1) * D_MODEL)
        gate_ref[0, :, c * D_MODEL:(c + 1) * D_MODEL] = jax.nn.sigmoid(gl).astype(BF16)


def _mix_in(x, lw, tables, tm):
    B, S, _ = x.shape
    cos, sina, sinb = tables
    row = lambda w: pl.BlockSpec((1, tm, w), lambda b, i: (b, i, 0))
    heads = lambda n: pl.BlockSpec((1, n, tm, LANES), lambda b, i: (b, 0, i, 0))
    tab = pl.BlockSpec((tm, LANES), lambda b, i: (i, 0))
    return pl.pallas_call(
        _mix_in_kernel,
        grid=(B, S // tm),
        in_specs=[row(D_MODEL), _const_spec((1, D_MODEL)), _const_spec((D_MODEL, IN_PACKED)),
                  _const_spec((1, Q_LORA)), _const_spec((Q_LORA, MLA_HEADS * LANES)),
                  _const_spec((1, KV_LORA)), _const_spec((KV_LORA, MLA_HEADS * LANES)),
                  _const_spec((KV_LORA, MLA_HEADS * LANES)),
                  tab, tab, tab,
                  _const_spec((1, LANES)), _const_spec((1, LANES)), _const_spec((1, LANES))],
        out_specs=[row(POOL_WIDTH), heads(MLA_HEADS), heads(MLA_HEADS), heads(MLA_HEADS),
                   heads(MEM_HEADS), row(N_BRANCH * D_MODEL)],
        out_shape=[jax.ShapeDtypeStruct((B, S, POOL_WIDTH), BF16),
                   jax.ShapeDtypeStruct((B, MLA_HEADS, S, LANES), BF16),
                   jax.ShapeDtypeStruct((B, MLA_HEADS, S, LANES), BF16),
                   jax.ShapeDtypeStruct((B, MLA_HEADS, S, LANES), BF16),
                   jax.ShapeDtypeStruct((B, MEM_HEADS, S, LANES), BF16),
                   jax.ShapeDtypeStruct((B, S, N_BRANCH * D_MODEL), BF16)],
        compiler_params=_params("parallel", "parallel"),
        name="mix_in",
    )(x, lw["g_mix"], lw["w_in"], lw["g_cq"], lw["w_uq"], lw["g_ckv"], lw["w_uk"], lw["w_uv"],
      cos, sina, sinb, lw["g_qn_mla"], lw["g_kn_mla"], lw["g_qn_mem"])


def _mem_kv_kernel(mem_ref, gmem_ref, w_ref, gkn_ref, km_ref, vm_ref):
    hn = (_rms(mem_ref[0], D_MODEL) * gmem_ref[...]).astype(BF16)
    kv = _dot(hn, w_ref[...])
    for h in range(MEM_HEADS):
        kh = kv[:, h * LANES:(h + 1) * LANES]
        km_ref[0, h] = (_rms(kh, MEM_HEAD_DIM) * gkn_ref[...]).astype(BF16)
        vm_ref[0, h] = kv[:, MEM_WIDTH + h * LANES:MEM_WIDTH + (h + 1) * LANES].astype(BF16)


def _mem_kv(mem, lw):
    B, M, _ = mem.shape
    out = pl.BlockSpec((1, MEM_HEADS, M, LANES), lambda b: (b, 0, 0, 0))
    shape = jax.ShapeDtypeStruct((B, MEM_HEADS, M, LANES), BF16)
    return pl.pallas_call(
        _mem_kv_kernel,
        grid=(B,),
        in_specs=[pl.BlockSpec((1, M, D_MODEL), lambda b: (b, 0, 0)), _const_spec((1, D_MODEL)),
                  _const_spec((D_MODEL, 2 * MEM_WIDTH)), _const_spec((1, LANES))],
        out_specs=[out, out],
        out_shape=[shape, shape],
        compiler_params=_params("parallel"),
        name="mem_kv",
    )(mem, lw["g_mem"], lw["w_mem_kv"], lw["g_kn_mem"])


def _pool_kernel(seq_len, tp, cur_ref, prev_ref, next_ref, pw_ref, ps_ref, out_ref):
    r0 = pl.program_id(1) * tp
    t = r0 + lax.broadcasted_iota(jnp.int32, (tp, 1), 0)

    def band(col0, ncols, lo, hi):
        s = col0 + lax.broadcasted_iota(jnp.int32, (tp, ncols), 1)
        return jnp.where(s >= lo, jnp.where(s < hi, 1.0, 0.0), 0.0).astype(BF16)

    for g, w in enumerate(POOL_WINDOWS):
        sl = slice(g * POOL_GROUP_DIM, (g + 1) * POOL_GROUP_DIM)
        lo = jnp.maximum(t - w // 2, 0)
        hi = jnp.minimum(t + w // 2, seq_len)
        u = cur_ref[0, :, sl]
        wsum = (_dot(band(r0, tp, lo, hi), u)
                + _dot(band(r0 - POOL_HALO, POOL_HALO, lo, hi), prev_ref[0, :, sl])
                + _dot(band(r0 + tp, POOL_HALO, lo, hi), next_ref[0, :, sl]))
        d = wsum / (hi - lo).astype(F32) - u.astype(F32)
        y = _dot(d.astype(BF16), pw_ref[g]) * ps_ref[:, sl]
        out_ref[0, :, sl] = y.astype(BF16)


def _pool(up, lw, tp):
    B, S, _ = up.shape
    per = tp // POOL_HALO
    last = S // POOL_HALO - 1
    return pl.pallas_call(
        functools.partial(_pool_kernel, S, tp),
        grid=(B, S // tp),
        in_specs=[pl.BlockSpec((1, tp, POOL_WIDTH), lambda b, i: (b, i, 0)),
                  pl.BlockSpec((1, POOL_HALO, POOL_WIDTH), lambda b, i: (b, jnp.maximum(i * per - 1, 0), 0)),
                  pl.BlockSpec((1, POOL_HALO, POOL_WIDTH), lambda b, i: (b, jnp.minimum((i + 1) * per, last), 0)),
                  _const_spec((N_POOL_GROUPS, POOL_GROUP_DIM, POOL_GROUP_DIM)),
                  _const_spec((1, POOL_WIDTH))],
        out_specs=pl.BlockSpec((1, tp, POOL_WIDTH), lambda b, i: (b, i, 0)),
        out_shape=jax.ShapeDtypeStruct((B, S, POOL_WIDTH), BF16),
        compiler_params=_params("parallel", "parallel"),
        name="pool",
    )(up, up, up, lw["pool_w"], lw["pool_scale"])


def _softmax_pv(s, v):
    m = jnp.max(s, axis=-1, keepdims=True)
    p = jnp.exp(s - m)
    l = jnp.sum(p, axis=-1, keepdims=True)
    return _dot(p.astype(BF16), v) / l


def _mla_attn_kernel(seq_len, tk, q_ref, k_ref, v_ref, o_ref):
    outs = []
    for hh in range(2):
        q = q_ref[0, hh]
        m = acc = None
        for c in range(seq_len // tk):
            ks = slice(c * tk, (c + 1) * tk)
            s = lax.dot_general(q, k_ref[0, hh, ks, :], NT_DIMS, preferred_element_type=F32)
            mc = jnp.max(s, axis=-1, keepdims=True)
            m_new = mc if m is None else jnp.maximum(m, mc)
            pv = _dot(jnp.exp2(s - m_new).astype(BF16), v_ref[0, hh, ks, :])
            acc = pv if acc is None else acc * jnp.exp2(m - m_new) + pv
            m = m_new
        outs.append(acc * (1.0 / acc[:, V_DIM:V_DIM + 1]))
    lane = lax.broadcasted_iota(jnp.int32, outs[0].shape, 1)
    o_ref[0] = jnp.where(lane < V_DIM, outs[0], pltpu.roll(outs[1], V_DIM, 1)).astype(BF16)


def _mla_attn(q, k, v, tq, tk):
    B, H, S, _ = q.shape
    kv = pl.BlockSpec((1, 2, S, LANES), lambda b, p, i: (b, p, 0, 0))
    return pl.pallas_call(
        functools.partial(_mla_attn_kernel, S, tk),
        grid=(B, H // 2, S // tq),
        in_specs=[pl.BlockSpec((1, 2, tq, LANES), lambda b, p, i: (b, p, i, 0)), kv, kv],
        out_specs=pl.BlockSpec((1, tq, 2 * V_DIM), lambda b, p, i: (b, i, p)),
        out_shape=jax.ShapeDtypeStruct((B, S, H * V_DIM), BF16),
        compiler_params=_params("parallel", "parallel", "arbitrary"),
        name="mla_attn",
    )(q, k, v)


def _merge_kernel(is_moe, yp_ref, ym_ref, qm_ref, km_ref, vm_ref, gate_ref, wbr_ref, wout_ref,
                  x_ref, gffn_ref, *rest):
    if is_moe:
        wrh_ref, wrl_ref, xo_ref, *hf_refs, ridx_ref, rw_ref = rest
    else:
        xo_ref, hf_ref = rest

    mem = []
    for h in range(MEM_HEADS):
        s = lax.dot_general(qm_ref[0, h], km_ref[0, h], NT_DIMS, preferred_element_type=F32)
        mem.append(_softmax_pv(s, vm_ref[0, h]).astype(BF16))
    y_mem = jnp.concatenate(mem, axis=-1)

    branches = (yp_ref[0], ym_ref[0], y_mem)
    merged = None
    for c, y in enumerate(branches):
        term = gate_ref[0, :, c * D_MODEL:(c + 1) * D_MODEL].astype(F32) * _dot(y, wbr_ref[c])
        merged = term if merged is None else merged + term
    x_new = x_ref[0] + _dot(merged.astype(BF16), wout_ref[...])
    xo_ref[0] = x_new
    hf = _rms(x_new, D_MODEL) * gffn_ref[...]
    if not is_moe:
        hf_ref[0] = hf.astype(BF16)
    else:
        for c, ref in enumerate(hf_refs):
            ref[0] = hf[:, c * SC_COLS:(c + 1) * SC_COLS]
        hf_hi = hf.astype(BF16)
        hf_lo = (hf - hf_hi.astype(F32)).astype(BF16)
        logits = _dot(hf_hi, wrh_ref[...]) + _dot(hf_lo, wrh_ref[...]) + _dot(hf_hi, wrl_ref[...])
        lane = lax.broadcasted_iota(jnp.int32, logits.shape, 1)
        lg = jnp.where(lane < N_EXPERTS, logits, -jnp.inf)
        m1 = jnp.max(lg, axis=-1, keepdims=True)
        i1 = jnp.min(jnp.where(lg == m1, lane, LANES), axis=-1, keepdims=True)
        lg2 = jnp.where(lane == i1, -jnp.inf, lg)
        m2 = jnp.max(lg2, axis=-1, keepdims=True)
        i2 = jnp.min(jnp.where(lg2 == m2, lane, LANES), axis=-1, keepdims=True)
        e = jnp.exp(m2 - m1)
        w1 = 1.0 / (1.0 + e)
        ridx_ref[0] = jnp.where(lane == 0, i1, jnp.where(lane == 1, i2, 0))
        rw_ref[0] = jnp.where(lane == 0, w1, jnp.where(lane == 1, e * w1, 0.0))


def _merge(is_moe, yp, ym, qm, km, vm, gate, x, lw, tm):
    B, S, _ = x.shape
    M = km.shape[2]
    row = lambda w: pl.BlockSpec((1, tm, w), lambda b, i: (b, i, 0))
    memkv = pl.BlockSpec((1, MEM_HEADS, M, LANES), lambda b, i: (b, 0, 0, 0))
    in_specs = [row(POOL_WIDTH), row(MLA_HEADS * V_DIM),
                pl.BlockSpec((1, MEM_HEADS, tm, LANES), lambda b, i: (b, 0, i, 0)), memkv, memkv,
                row(N_BRANCH * D_MODEL), _const_spec((N_BRANCH, POOL_WIDTH, D_MODEL)),
                _const_spec((D_MODEL, D_MODEL)), row(D_MODEL), _const_spec((1, D_MODEL))]
    args = [yp, ym, qm, km, vm, gate, lw["w_br"], lw["w_out"], x, lw["g_ffn"]]
    out_specs = [row(D_MODEL)]
    out_shape = [jax.ShapeDtypeStruct((B, S, D_MODEL), F32)]
    if is_moe:
        in_specs += [_const_spec((D_MODEL, LANES)), _const_spec((D_MODEL, LANES))]
        args += [lw["w_router_hi"], lw["w_router_lo"]]
        out_specs += [row(SC_COLS)] * SC_CHUNKS + [row(LANES), row(LANES)]
        out_shape += ([jax.ShapeDtypeStruct((B, S, SC_COLS), F32)] * SC_CHUNKS
                      + [jax.ShapeDtypeStruct((B, S, LANES), jnp.int32), jax.ShapeDtypeStruct((B, S, LANES), F32)])
    else:
        out_specs.append(row(D_MODEL))
        out_shape.append(jax.ShapeDtypeStruct((B, S, D_MODEL), BF16))
    return pl.pallas_call(
        functools.partial(_merge_kernel, is_moe),
        grid=(B, S // tm),
        in_specs=in_specs, out_specs=out_specs, out_shape=out_shape,
        compiler_params=_params("parallel", "parallel"),
        name="merge_moe" if is_moe else "merge",
    )(*args)


def _swiglu_chunk(h, wg, wu):
    g = _dot(h, wg)
    return g * jax.nn.sigmoid(g) * _dot(h, wu)


def _swiglu_down(h, wgu_ref, wd_ref, acc):
    for j in range(D_FF // FF_CHUNK):
        lo, hi = j * FF_CHUNK, (j + 1) * FF_CHUNK
        a = _swiglu_chunk(h, wgu_ref[:, lo:hi], wgu_ref[:, D_FF + lo:D_FF + hi])
        acc = acc + _dot(a.astype(BF16), wd_ref[lo:hi, :])
    return acc


def _ffn_kernel(hf_ref, wgu_ref, wd_ref, x_ref, o_ref):
    o_ref[...] = _swiglu_down(hf_ref[...], wgu_ref, wd_ref, x_ref[...])


def _ffn(hf, x, w_gu, w_down, tm):
    N = hf.shape[0]
    row = pl.BlockSpec((tm, D_MODEL), lambda i: (i, 0))
    return pl.pallas_call(
        _ffn_kernel,
        grid=(N // tm,),
        in_specs=[row, _const_spec((D_MODEL, 2 * D_FF)), _const_spec((D_FF, D_MODEL)), row],
        out_specs=row,
        out_shape=jax.ShapeDtypeStruct((N, D_MODEL), F32),
        compiler_params=_params("parallel"),
        name="ffn",
    )(hf, w_gu, w_down, x)


def _route_plan(ridx, tmx):
    n = ridx.shape[0]
    e = ridx[:, :2].T.reshape(2 * n)
    onehot = (e[:, None] == jnp.arange(N_EXPERTS, dtype=jnp.int32)[None, :]).astype(jnp.int32)
    csum = jnp.cumsum(onehot, axis=0)
    rank = jnp.sum((csum - onehot) * onehot, axis=-1)
    counts = csum[-1]
    padded = ((counts + tmx - 1) // tmx) * tmx
    ends = jnp.cumsum(padded)
    starts = ends - padded
    dest = jnp.sum(onehot * starts[None, :], axis=-1) + rank
    n_tiles = (2 * n) // tmx + N_EXPERTS
    tile_row0 = jnp.arange(n_tiles, dtype=jnp.int32) * tmx
    tile_e = jnp.minimum(jnp.sum((tile_row0[:, None] >= ends[None, :]).astype(jnp.int32), axis=-1),
                         N_EXPERTS - 1)
    group_end = jnp.take(starts + counts, tile_e)
    tile_valid = jnp.where(tile_row0 < ends[-1], jnp.clip(group_end - tile_row0, 0, tmx), 0)
    return dest.reshape(2, n).astype(jnp.int32), tile_e.astype(jnp.int32), tile_valid.astype(jnp.int32), n_tiles * tmx


def _sc_mesh():
    return plsc.VectorSubcoreMesh(core_axis_name="core", subcore_axis_name="subcore")


def _sc_scatter_rows(chunks, dest, n_out):
    n, width = chunks[0].shape
    per_choice = n // SC_WINDOW
    out_type = [jax.ShapeDtypeStruct((n_out, width), chunks[0].dtype)] * len(chunks)

    @pl.kernel(out_type=out_type, mesh=_sc_mesh(), scratch_types=[])
    def scatter(*refs):
        x_refs, i_hbm, o_refs = refs[:len(chunks)], refs[len(chunks)], refs[len(chunks) + 1:]
        for x_hbm, o_hbm in zip(x_refs, o_refs):
            def body(x_vmem, i_vmem, o_hbm=o_hbm):
                pltpu.sync_copy(x_vmem, o_hbm.at[i_vmem.at[0]])

            pltpu.emit_pipeline(
                body,
                grid=(2 * per_choice,),
                in_specs=[pl.BlockSpec((SC_WINDOW, width), lambda i: (i % per_choice, 0)),
                          pl.BlockSpec((1, SC_WINDOW), lambda i: (0, i))],
                out_specs=[],
                core_axis_name=("core", "subcore"),
                dimension_semantics=(pltpu.PARALLEL,),
            )(x_hbm, i_hbm)

    return scatter(*chunks, dest.reshape(1, 2 * n))


def _sc_gather_rows(chunks, idx):
    n = idx.shape[0]
    width = chunks[0].shape[1]
    out_type = [jax.ShapeDtypeStruct((n, width), chunks[0].dtype)] * len(chunks)

    @pl.kernel(out_type=out_type, mesh=_sc_mesh(), scratch_types=[])
    def gather(*refs):
        x_refs, i_hbm, o_refs = refs[:len(chunks)], refs[len(chunks)], refs[len(chunks) + 1:]
        for x_hbm, o_hbm in zip(x_refs, o_refs):
            def body(i_vmem, o_vmem, x_hbm=x_hbm):
                pltpu.sync_copy(x_hbm.at[i_vmem.at[0]], o_vmem)

            pltpu.emit_pipeline(
                body,
                grid=(n // SC_WINDOW,),
                in_specs=[pl.BlockSpec((1, SC_WINDOW), lambda i: (0, i))],
                out_specs=[pl.BlockSpec((SC_WINDOW, width), lambda i: (i, 0))],
                core_axis_name=("core", "subcore"),
                dimension_semantics=(pltpu.PARALLEL,),
            )(i_hbm, o_hbm)

    return gather(*chunks, idx.reshape(1, n))


def _moe_group_kernel(te_ref, tv_ref, *refs):
    xs_refs, (wgu_ref, wd_ref) = refs[:SC_CHUNKS], refs[SC_CHUNKS:SC_CHUNKS + 2]
    ys_refs = refs[SC_CHUNKS + 2:]
    valid = tv_ref[pl.program_id(0)]

    @pl.when(valid > 0)
    def _():
        row = lax.broadcasted_iota(jnp.int32, xs_refs[0].shape, 0)
        h = jnp.concatenate([jnp.where(row < valid, xs_ref[...], 0.0).astype(BF16) for xs_ref in xs_refs], axis=-1)
        y = _swiglu_down(h, wgu_ref.at[0], wd_ref.at[0], jnp.zeros((h.shape[0], D_MODEL), F32))
        for c, ys_ref in enumerate(ys_refs):
            ys_ref[...] = y[:, c * SC_COLS:(c + 1) * SC_COLS]


def _moe_group(xs, tile_e, tile_valid, w_gu, w_down, tmx):
    n_rows = xs[0].shape[0]
    chunk = pl.BlockSpec((tmx, SC_COLS), lambda t, te, tv: (t, 0))
    grid_spec = pltpu.PrefetchScalarGridSpec(
        num_scalar_prefetch=2,
        grid=(n_rows // tmx,),
        in_specs=[chunk] * SC_CHUNKS + [
            pl.BlockSpec((1, D_MODEL, 2 * D_FF), lambda t, te, tv: (te[t], 0, 0)),
            pl.BlockSpec((1, D_FF, D_MODEL), lambda t, te, tv: (te[t], 0, 0))],
        out_specs=[chunk] * SC_CHUNKS,
    )
    return pl.pallas_call(
        _moe_group_kernel,
        grid_spec=grid_spec,
        out_shape=[jax.ShapeDtypeStruct((n_rows, SC_COLS), F32)] * SC_CHUNKS,
        compiler_params=_params("arbitrary"),
        name="moe_group",
    )(tile_e, tile_valid, *xs, w_gu, w_down)


def _moe_combine_kernel(x_ref, rw_ref, *refs):
    y_refs, o_ref = refs[:-1], refs[-1]
    rw = rw_ref[...]
    w0, w1 = rw[:, 0:1], rw[:, 1:2]
    for c in range(SC_CHUNKS):
        sl = slice(c * SC_COLS, (c + 1) * SC_COLS)
        o_ref[:, sl] = x_ref[:, sl] + w0 * y_refs[2 * c][...] + w1 * y_refs[2 * c + 1][...]


def _moe_combine(x, yg, rw, tm):
    n = x.shape[0]
    per_choice = n // tm
    row = pl.BlockSpec((tm, D_MODEL), lambda i: (i, 0))
    first = pl.BlockSpec((tm, SC_COLS), lambda i: (i, 0))
    second = pl.BlockSpec((tm, SC_COLS), lambda i: (i + per_choice, 0))
    args = []
    for y in yg:
        args += [y, y]
    return pl.pallas_call(
        _moe_combine_kernel,
        grid=(n // tm,),
        in_specs=[row, pl.BlockSpec((tm, LANES), lambda i: (i, 0))] + [first, second] * SC_CHUNKS,
        out_specs=row,
        out_shape=jax.ShapeDtypeStruct((n, D_MODEL), F32),
        compiler_params=_params("parallel"),
        name="moe_combine",
    )(x, rw, *args)


def _moe(hf_chunks, ridx, rw, x, w_gu, w_down, tm):
    n = x.shape[0]
    dest, tile_e, tile_valid, n_rows = _route_plan(ridx, MOE_TILE)
    xs = _sc_scatter_rows(hf_chunks, dest, n_rows)
    ys = _moe_group(xs, tile_e, tile_valid, w_gu, w_down, MOE_TILE)
    yg = _sc_gather_rows(ys, dest.reshape(2 * n))
    return _moe_combine(x, yg, rw, tm)


def _rope_tables(seq_len):
    half = QK_ROPE // 2
    freqs = jnp.power(ROPE_THETA, -jnp.arange(half, dtype=F32) / half)
    ang = jnp.arange(seq_len).astype(F32)[:, None] * freqs[None, :]
    c, s = jnp.cos(ang), jnp.sin(ang)
    z = lambda n: jnp.zeros((seq_len, n), F32)
    o = lambda n: jnp.ones((seq_len, n), F32)
    tail = LANES - QK_NOPE - QK_ROPE
    cos = jnp.concatenate([o(QK_NOPE), c, c, o(tail)], axis=-1)
    sina = jnp.concatenate([z(QK_NOPE), -s, z(half), z(tail)], axis=-1)
    sinb = jnp.concatenate([z(QK_NOPE), z(half), s, z(tail)], axis=-1)
    return cos, sina, sinb


def _pad_last(a, width):
    return jnp.pad(a, [(0, 0)] * (a.ndim - 1) + [(0, width - a.shape[-1])])


def _layer_weights(l, g_mix, w_in, pool_w, pool_scale, g_cq, w_uq, g_ckv, w_ukv, g_qn_mla, g_kn_mla,
                   g_mem, w_mem_kv, g_qn_mem, g_kn_mem, w_br, w_out, g_ffn, w_router):
    wi = w_in[l]
    kr_lo, kr_hi = POOL_WIDTH + Q_LORA + KV_LORA, POOL_WIDTH + Q_LORA + KV_LORA + QK_ROPE
    zc = lambda n: jnp.zeros((D_MODEL, n), F32)
    w_in_p = jnp.concatenate([wi[:, :kr_lo], zc(QK_NOPE), wi[:, kr_lo:kr_hi], zc(LANES - QK_NOPE - QK_ROPE),
                              wi[:, kr_hi:]], axis=-1)
    w_uq_p = _pad_last(w_uq[l].reshape(Q_LORA, MLA_HEADS, QK_DIM), LANES).reshape(Q_LORA, MLA_HEADS * LANES)
    ukv = w_ukv[l].reshape(KV_LORA, MLA_HEADS, QK_NOPE + V_DIM)
    w_uk_p = _pad_last(ukv[..., :QK_NOPE], LANES).reshape(KV_LORA, MLA_HEADS * LANES)
    w_uv_p = _pad_last(ukv[..., QK_NOPE:], LANES).reshape(KV_LORA, MLA_HEADS * LANES)
    row = lambda a: a.reshape(1, -1)
    lw = {
        "g_mix": row(g_mix[l]), "w_in": w_in_p.astype(BF16),
        "g_cq": row(g_cq[l]), "w_uq": w_uq_p.astype(BF16),
        "g_ckv": row(g_ckv[l]), "w_uk": w_uk_p.astype(BF16), "w_uv": w_uv_p.astype(BF16),
        "g_qn_mla": _pad_last(row(g_qn_mla[l]), LANES), "g_kn_mla": _pad_last(row(g_kn_mla[l]), LANES),
        "g_qn_mem": row(g_qn_mem[l]), "g_kn_mem": row(g_kn_mem[l]),
        "g_mem": row(g_mem[l]), "w_mem_kv": w_mem_kv[l].astype(BF16),
        "pool_w": pool_w[l].astype(BF16), "pool_scale": row(pool_scale[l]),
        "w_br": w_br[l].astype(BF16), "w_out": w_out[l].astype(BF16), "g_ffn": row(g_ffn[l]),
    }
    if l % 2 == 1:
        wr = _pad_last(w_router[l // 2], LANES)
        lw["w_router_hi"] = wr.astype(BF16)
        lw["w_router_lo"] = (wr - lw["w_router_hi"].astype(F32)).astype(BF16)
    return lw


def _trunk(x, mem, layers, ffn_w):
    B, S, _ = x.shape
    tm = min(512, S)
    tq = min(1024, S)
    tk = min(max(S // 2, 1024), S)
    tp = min(256, S)
    tn = min(512, B * S)
    tables = _rope_tables(S)
    for l, lw in enumerate(layers):
        up, q, k, v, qm, gate = _mix_in(x, lw, tables, tm)
        km, vm = _mem_kv(mem, lw)
        yp = _pool(up, lw, tp)
        ym = _mla_attn(q, k, v, tq, tk)
        w_gu, w_down = ffn_w[l]
        if l % 2 == 0:
            x_mid, hf = _merge(False, yp, ym, qm, km, vm, gate, x, lw, tm)
            x = _ffn(hf.reshape(B * S, D_MODEL), x_mid.reshape(B * S, D_MODEL), w_gu, w_down, tn)
        else:
            x_mid, *hf, ridx, rw = _merge(True, yp, ym, qm, km, vm, gate, x, lw, tm)
            x = _moe([h.reshape(B * S, SC_COLS) for h in hf], ridx.reshape(B * S, LANES),
                     rw.reshape(B * S, LANES), x_mid.reshape(B * S, D_MODEL), w_gu, w_down, tn)
        x = x.reshape(B, S, D_MODEL)
    return x


def kernel(x_prompt, x_sample, mem_prompt, mem_sample, g_mix, w_in, pool_w, pool_scale, g_cq, w_uq, g_ckv, w_ukv, g_qn_mla, g_kn_mla, g_mem, w_mem_kv, g_qn_mem, g_kn_mem, w_br, w_out, g_ffn, w_gu_dense, w_down_dense, w_router, w_gu_moe, w_down_moe):
    layers = [_layer_weights(l, g_mix, w_in, pool_w, pool_scale, g_cq, w_uq, g_ckv, w_ukv, g_qn_mla,
                             g_kn_mla, g_mem, w_mem_kv, g_qn_mem, g_kn_mem, w_br, w_out, g_ffn, w_router)
              for l in range(DEPTH)]
    ffn_w = []
    for l in range(DEPTH):
        if l % 2 == 0:
            ffn_w.append((w_gu_dense[l // 2].astype(BF16), w_down_dense[l // 2].astype(BF16)))
        else:
            ffn_w.append((w_gu_moe[l // 2].astype(BF16), w_down_moe[l // 2].astype(BF16)))
    y_prompt = _trunk(x_prompt, mem_prompt, layers, ffn_w)
    y_sample = _trunk(x_sample, mem_sample, layers, ffn_w)
    return (y_prompt, y_sample)
```

```python
import functools

import numpy as np

import jax
import jax.numpy as jnp
from jax import lax
from jax.experimental import pallas as pl
from jax.experimental.pallas import tpu as pltpu
from jax.experimental.pallas import tpu_sc as plsc

D_MODEL = 1024
DEPTH = 4
POOL_WINDOWS = (2, 4, 8, 16)
N_POOL_GROUPS = 4
POOL_GROUP_DIM = 128
POOL_WIDTH = 512
MLA_HEADS = 8
QK_NOPE = 64
QK_ROPE = 32
QK_DIM = 96
V_DIM = 64
Q_LORA = 384
KV_LORA = 256
ROPE_THETA = 10000.0
MEM_TOKENS = 256
MEM_HEADS = 4
MEM_HEAD_DIM = 128
MEM_WIDTH = 512
N_BRANCH = 3
D_FF = 2816
N_EXPERTS = 8
EPS = 1e-6
LOG2_E = 1.4426950408889634

LANES = 128
POOL_HALO = 16
VMEM_LIMIT_BYTES = 56 * 1024 * 1024
MOE_TILE = 512
FF_CHUNK = 256
SC_WINDOW = 128
SC_COLS = 256
SC_CHUNKS = D_MODEL // SC_COLS

C_POOL = 0
C_CQ = C_POOL + POOL_WIDTH
C_CKV = C_CQ + Q_LORA
C_KR = C_CKV + KV_LORA
C_QM = C_KR + LANES
C_GATE = C_QM + MEM_WIDTH
IN_PACKED = C_GATE + N_BRANCH * D_MODEL

BF16 = jnp.bfloat16
F32 = jnp.float32
NT_DIMS = (((1,), (1,)), ((), ()))


def _const_spec(shape):
    zeros = (0,) * len(shape)
    return pl.BlockSpec(shape, lambda *_: zeros, pipeline_mode=pl.Buffered(1))


def _layer_spec(l, shape):
    zeros = (0,) * len(shape)
    return pl.BlockSpec((None,) + tuple(shape), lambda *_: (l,) + zeros, pipeline_mode=pl.Buffered(1))


def _params(*sem):
    return pltpu.CompilerParams(dimension_semantics=sem, vmem_limit_bytes=VMEM_LIMIT_BYTES)


def _rms(x, width):
    return x * lax.rsqrt(jnp.sum(x * x, axis=-1, keepdims=True) * (1.0 / width) + EPS)


def _dot(a, b):
    return jnp.dot(a, b, preferred_element_type=F32)


def _mix_in_kernel(x_ref, gmix_ref, win_ref, gcq_ref, wuq_ref, gckv_ref, wuk_ref, wuv_ref,
                   cos_ref, sina_ref, sinb_ref, gqn_ref, gkn_ref, gqm_ref,
                   up_ref, q_ref, k_ref, v_ref, qm_ref, gate_ref):
    hn = (_rms(x_ref[0], D_MODEL) * gmix_ref[...]).astype(BF16)

    def proj(lo, hi):
        return _dot(hn, win_ref[:, lo:hi])

    cos, sina, sinb = cos_ref[...], sina_ref[...], sinb_ref[...]

    def rope(t):
        return t * cos + pltpu.roll(t, LANES - QK_ROPE // 2, 1) * sina + pltpu.roll(t, QK_ROPE // 2, 1) * sinb

    up_ref[0] = proj(C_POOL, C_CQ).astype(BF16)

    cqn = (_rms(proj(C_CQ, C_CKV), Q_LORA) * gcq_ref[...]).astype(BF16)
    qf = _dot(cqn, wuq_ref[...])
    gq = gqn_ref[...] * (QK_DIM ** -0.5 * LOG2_E)
    for h in range(MLA_HEADS):
        qh = rope(qf[:, h * LANES:(h + 1) * LANES])
        q_ref[0, h] = (_rms(qh, QK_DIM) * gq).astype(BF16)

    ckvn = (_rms(proj(C_CKV, C_KR), KV_LORA) * gckv_ref[...]).astype(BF16)
    kf = _dot(ckvn, wuk_ref[...])
    kr = rope(proj(C_KR, C_QM))
    gk = gkn_ref[...]
    for h in range(MLA_HEADS):
        kh = kf[:, h * LANES:(h + 1) * LANES] + kr
        k_ref[0, h] = (_rms(kh, QK_DIM) * gk).astype(BF16)
    vf = _dot(ckvn, wuv_ref[...])
    lane = lax.broadcasted_iota(jnp.int32, (vf.shape[0], LANES), 1)
    for h in range(MLA_HEADS):
        v_ref[0, h] = jnp.where(lane < V_DIM, vf[:, h * LANES:(h + 1) * LANES], 1.0).astype(BF16)

    qm = proj(C_QM, C_GATE)
    gm = gqm_ref[...] * (MEM_HEAD_DIM ** -0.5)
    for h in range(MEM_HEADS):
        qh = qm[:, h * LANES:(h + 1) * LANES]
        qm_ref[0, h] = (_rms(qh, MEM_HEAD_DIM) * gm).astype(BF16)

    for c in range(N_BRANCH):
        gl = proj(C_GATE + c * D_MODEL, C_GATE + (c + 1) * D_MODEL)
        gate_ref[0, :, c * D_MODEL:(c + 1) * D_MODEL] = jax.nn.sigmoid(gl).astype(BF16)


def _mix_in(x, l, W, tables, tm):
    B, S, _ = x.shape
    cos, sina, sinb = tables
    row = lambda w: pl.BlockSpec((1, tm, w), lambda b, i: (b, i, 0))
    heads = lambda n: pl.BlockSpec((1, n, tm, LANES), lambda b, i: (b, 0, i, 0))
    tab = pl.BlockSpec((tm, LANES), lambda b, i: (i, 0))
    ls = functools.partial(_layer_spec, l)
    head_shape = jax.ShapeDtypeStruct((B, MLA_HEADS, S, LANES), BF16)
    return pl.pallas_call(
        _mix_in_kernel,
        grid=(B, S // tm),
        in_specs=[row(D_MODEL), ls((1, D_MODEL)), ls((D_MODEL, IN_PACKED)),
                  ls((1, Q_LORA)), ls((Q_LORA, MLA_HEADS * LANES)),
                  ls((1, KV_LORA)), ls((KV_LORA, MLA_HEADS * LANES)), ls((KV_LORA, MLA_HEADS * LANES)),
                  tab, tab, tab,
                  ls((1, LANES)), ls((1, LANES)), ls((1, LANES))],
        out_specs=[row(POOL_WIDTH), heads(MLA_HEADS), heads(MLA_HEADS), heads(MLA_HEADS),
                   heads(MEM_HEADS), row(N_BRANCH * D_MODEL)],
        out_shape=[jax.ShapeDtypeStruct((B, S, POOL_WIDTH), BF16), head_shape, head_shape, head_shape,
                   jax.ShapeDtypeStruct((B, MEM_HEADS, S, LANES), BF16),
                   jax.ShapeDtypeStruct((B, S, N_BRANCH * D_MODEL), BF16)],
        compiler_params=_params("parallel", "parallel"),
        name="mix_in",
    )(x, W["g_mix"], W["w_in"], W["g_cq"], W["w_uq"], W["g_ckv"], W["w_uk"], W["w_uv"],
      cos, sina, sinb, W["g_qn_mla"], W["g_kn_mla"], W["g_qn_mem"])


def _mem_kv_kernel(mem_ref, gmem_ref, w_ref, gkn_ref, km_ref, vm_ref):
    hn = (_rms(mem_ref[0], D_MODEL) * gmem_ref[...]).astype(BF16)
    kv = _dot(hn, w_ref[...])
    for h in range(MEM_HEADS):
        kh = kv[:, h * LANES:(h + 1) * LANES]
        km_ref[0, h] = (_rms(kh, MEM_HEAD_DIM) * gkn_ref[...]).astype(BF16)
        vm_ref[0, h] = kv[:, MEM_WIDTH + h * LANES:MEM_WIDTH + (h + 1) * LANES].astype(BF16)


def _mem_kv(mem, l, W):
    B, M, _ = mem.shape
    out = pl.BlockSpec((1, MEM_HEADS, M, LANES), lambda b: (b, 0, 0, 0))
    shape = jax.ShapeDtypeStruct((B, MEM_HEADS, M, LANES), BF16)
    return pl.pallas_call(
        _mem_kv_kernel,
        grid=(B,),
        in_specs=[pl.BlockSpec((1, M, D_MODEL), lambda b: (b, 0, 0)), _layer_spec(l, (1, D_MODEL)),
                  _layer_spec(l, (D_MODEL, 2 * MEM_WIDTH)), _layer_spec(l, (1, LANES))],
        out_specs=[out, out],
        out_shape=[shape, shape],
        compiler_params=_params("parallel"),
        name="mem_kv",
    )(mem, W["g_mem"], W["w_mem_kv"], W["g_kn_mem"])


def _pool_bands(tp):
    i = np.arange(tp)[:, None]

    def member(j, w):
        return ((j - i >= -(w // 2)) & (j - i < w // 2)).astype(np.float32)

    cur = np.stack([member(np.arange(tp)[None, :], w) for w in POOL_WINDOWS])
    prev = np.stack([member(np.arange(-POOL_HALO, 0)[None, :], w) for w in POOL_WINDOWS])
    nxt = np.stack([member(np.arange(tp, tp + POOL_HALO)[None, :], w) for w in POOL_WINDOWS])
    return jnp.asarray(cur, BF16), jnp.asarray(prev, BF16), jnp.asarray(nxt, BF16)


def _pool_kernel(seq_len, tp, cur_ref, prev_ref, next_ref, bc_ref, bp_ref, bn_ref, pw_ref, ps_ref, out_ref):
    i = pl.program_id(1)
    t = i * tp + lax.broadcasted_iota(jnp.int32, (tp, 1), 0)
    has_prev = i > 0
    has_next = i < pl.num_programs(1) - 1
    for g, w in enumerate(POOL_WINDOWS):
        sl = slice(g * POOL_GROUP_DIM, (g + 1) * POOL_GROUP_DIM)
        u = cur_ref[0, :, sl]
        wsum = (_dot(bc_ref[g], u)
                + jnp.where(has_prev, _dot(bp_ref[g], prev_ref[0, :, sl]), 0.0)
                + jnp.where(has_next, _dot(bn_ref[g], next_ref[0, :, sl]), 0.0))
        count = jnp.minimum(t + w // 2, seq_len) - jnp.maximum(t - w // 2, 0)
        d = wsum / count.astype(F32) - u.astype(F32)
        y = _dot(d.astype(BF16), pw_ref[g]) * ps_ref[:, sl]
        out_ref[0, :, sl] = y.astype(BF16)


def _pool(up, l, W, tp):
    B, S, _ = up.shape
    per = tp // POOL_HALO
    last = S // POOL_HALO - 1
    bands = _pool_bands(tp)
    return pl.pallas_call(
        functools.partial(_pool_kernel, S, tp),
        grid=(B, S // tp),
        in_specs=[pl.BlockSpec((1, tp, POOL_WIDTH), lambda b, i: (b, i, 0)),
                  pl.BlockSpec((1, POOL_HALO, POOL_WIDTH), lambda b, i: (b, jnp.maximum(i * per - 1, 0), 0)),
                  pl.BlockSpec((1, POOL_HALO, POOL_WIDTH), lambda b, i: (b, jnp.minimum((i + 1) * per, last), 0)),
                  _const_spec((N_POOL_GROUPS, tp, tp)), _const_spec((N_POOL_GROUPS, tp, POOL_HALO)),
                  _const_spec((N_POOL_GROUPS, tp, POOL_HALO)),
                  _layer_spec(l, (N_POOL_GROUPS, POOL_GROUP_DIM, POOL_GROUP_DIM)),
                  _layer_spec(l, (1, POOL_WIDTH))],
        out_specs=pl.BlockSpec((1, tp, POOL_WIDTH), lambda b, i: (b, i, 0)),
        out_shape=jax.ShapeDtypeStruct((B, S, POOL_WIDTH), BF16),
        compiler_params=_params("parallel", "parallel"),
        name="pool",
    )(up, up, up, *bands, W["pool_w"], W["pool_scale"])


def _softmax_pv(s, v):
    m = jnp.max(s, axis=-1, keepdims=True)
    p = jnp.exp(s - m)
    l = jnp.sum(p, axis=-1, keepdims=True)
    return _dot(p.astype(BF16), v) / l


def _mla_attn_kernel(seq_len, tk, q_ref, k_ref, v_ref, o_ref):
    outs = []
    for hh in range(2):
        q = q_ref[0, hh]
        m = acc = None
        for c in range(seq_len // tk):
            ks = slice(c * tk, (c + 1) * tk)
            s = lax.dot_general(q, k_ref[0, hh, ks, :], NT_DIMS, preferred_element_type=F32)
            mc = jnp.max(s, axis=-1, keepdims=True)
            m_new = mc if m is None else jnp.maximum(m, mc)
            pv = _dot(jnp.exp2(s - m_new).astype(BF16), v_ref[0, hh, ks, :])
            acc = pv if acc is None else acc * jnp.exp2(m - m_new) + pv
            m = m_new
        outs.append(acc * (1.0 / acc[:, V_DIM:V_DIM + 1]))
    lane = lax.broadcasted_iota(jnp.int32, outs[0].shape, 1)
    o_ref[0] = jnp.where(lane < V_DIM, outs[0], pltpu.roll(outs[1], V_DIM, 1)).astype(BF16)


def _mla_attn(q, k, v, tq, tk):
    B, H, S, _ = q.shape
    kv = pl.BlockSpec((1, 2, S, LANES), lambda b, p, i: (b, p, 0, 0))
    return pl.pallas_call(
        functools.partial(_mla_attn_kernel, S, tk),
        grid=(B, H // 2, S // tq),
        in_specs=[pl.BlockSpec((1, 2, tq, LANES), lambda b, p, i: (b, p, i, 0)), kv, kv],
        out_specs=pl.BlockSpec((1, tq, 2 * V_DIM), lambda b, p, i: (b, i, p)),
        out_shape=jax.ShapeDtypeStruct((B, S, H * V_DIM), BF16),
        compiler_params=_params("parallel", "parallel", "arbitrary"),
        name="mla_attn",
    )(q, k, v)


def _merge_kernel(is_moe, yp_ref, ym_ref, qm_ref, km_ref, vm_ref, gate_ref, wbr_ref, wout_ref,
                  x_ref, gffn_ref, xo_ref, *hf_refs):
    mem = []
    for h in range(MEM_HEADS):
        s = lax.dot_general(qm_ref[0, h], km_ref[0, h], NT_DIMS, preferred_element_type=F32)
        mem.append(_softmax_pv(s, vm_ref[0, h]).astype(BF16))
    y_mem = jnp.concatenate(mem, axis=-1)

    branches = (yp_ref[0], ym_ref[0], y_mem)
    merged = None
    for c, y in enumerate(branches):
        term = gate_ref[0, :, c * D_MODEL:(c + 1) * D_MODEL].astype(F32) * _dot(y, wbr_ref[c])
        merged = term if merged is None else merged + term
    x_new = x_ref[0] + _dot(merged.astype(BF16), wout_ref[...])
    xo_ref[0] = x_new
    hf = _rms(x_new, D_MODEL) * gffn_ref[...]
    if is_moe:
        for c, ref in enumerate(hf_refs):
            ref[0] = hf[:, c * SC_COLS:(c + 1) * SC_COLS]
    else:
        hf_refs[0][0] = hf.astype(BF16)


def _merge(is_moe, yp, ym, qm, km, vm, gate, x, l, W, tm):
    B, S, _ = x.shape
    M = km.shape[2]
    row = lambda w: pl.BlockSpec((1, tm, w), lambda b, i: (b, i, 0))
    memkv = pl.BlockSpec((1, MEM_HEADS, M, LANES), lambda b, i: (b, 0, 0, 0))
    in_specs = [row(POOL_WIDTH), row(MLA_HEADS * V_DIM),
                pl.BlockSpec((1, MEM_HEADS, tm, LANES), lambda b, i: (b, 0, i, 0)), memkv, memkv,
                row(N_BRANCH * D_MODEL), _layer_spec(l, (N_BRANCH, POOL_WIDTH, D_MODEL)),
                _layer_spec(l, (D_MODEL, D_MODEL)), row(D_MODEL), _layer_spec(l, (1, D_MODEL))]
    out_specs = [row(D_MODEL)]
    out_shape = [jax.ShapeDtypeStruct((B, S, D_MODEL), F32)]
    if is_moe:
        out_specs += [row(SC_COLS)] * SC_CHUNKS
        out_shape += [jax.ShapeDtypeStruct((B, S, SC_COLS), F32)] * SC_CHUNKS
    else:
        out_specs.append(row(D_MODEL))
        out_shape.append(jax.ShapeDtypeStruct((B, S, D_MODEL), BF16))
    return pl.pallas_call(
        functools.partial(_merge_kernel, is_moe),
        grid=(B, S // tm),
        in_specs=in_specs, out_specs=out_specs, out_shape=out_shape,
        compiler_params=_params("parallel", "parallel"),
        name="merge_moe" if is_moe else "merge",
    )(yp, ym, qm, km, vm, gate, W["w_br"], W["w_out"], x, W["g_ffn"])


def _swiglu_chunk(h, wg, wu):
    g = _dot(h, wg)
    return g * jax.nn.sigmoid(g) * _dot(h, wu)


def _swiglu_down(h, wgu_ref, wd_ref, acc):
    for j in range(D_FF // FF_CHUNK):
        lo, hi = j * FF_CHUNK, (j + 1) * FF_CHUNK
        a = _swiglu_chunk(h, wgu_ref[:, lo:hi], wgu_ref[:, D_FF + lo:D_FF + hi])
        acc = acc + _dot(a.astype(BF16), wd_ref[lo:hi, :])
    return acc


def _ffn_kernel(hf_ref, wgu_ref, wd_ref, x_ref, o_ref):
    o_ref[...] = _swiglu_down(hf_ref[...], wgu_ref, wd_ref, x_ref[...])


def _ffn(hf, x, l, W, tm):
    N = hf.shape[0]
    row = pl.BlockSpec((tm, D_MODEL), lambda i: (i, 0))
    return pl.pallas_call(
        _ffn_kernel,
        grid=(N // tm,),
        in_specs=[row, _layer_spec(l // 2, (D_MODEL, 2 * D_FF)), _layer_spec(l // 2, (D_FF, D_MODEL)), row],
        out_specs=row,
        out_shape=jax.ShapeDtypeStruct((N, D_MODEL), F32),
        compiler_params=_params("parallel"),
        name="ffn",
    )(hf, W["w_gu_dense"], W["w_down_dense"], x)


def _router_kernel(*refs):
    hf_refs, (wrh_ref, wrl_ref, ridx_ref, rw_ref) = refs[:SC_CHUNKS], refs[SC_CHUNKS:]
    logits = None
    for c, hf_ref in enumerate(hf_refs):
        rows = slice(c * SC_COLS, (c + 1) * SC_COLS)
        hf = hf_ref[...]
        hi = hf.astype(BF16)
        lo = (hf - hi.astype(F32)).astype(BF16)
        part = _dot(hi, wrh_ref[rows, :]) + _dot(lo, wrh_ref[rows, :]) + _dot(hi, wrl_ref[rows, :])
        logits = part if logits is None else logits + part
    lane = lax.broadcasted_iota(jnp.int32, logits.shape, 1)
    lg = jnp.where(lane < N_EXPERTS, logits, -jnp.inf)
    m1 = jnp.max(lg, axis=-1, keepdims=True)
    i1 = jnp.min(jnp.where(lg == m1, lane, LANES), axis=-1, keepdims=True)
    lg2 = jnp.where(lane == i1, -jnp.inf, lg)
    m2 = jnp.max(lg2, axis=-1, keepdims=True)
    i2 = jnp.min(jnp.where(lg2 == m2, lane, LANES), axis=-1, keepdims=True)
    e = jnp.exp(m2 - m1)
    w1 = 1.0 / (1.0 + e)
    ridx_ref[...] = jnp.where(lane == 0, i1, jnp.where(lane == 1, i2, 0))
    rw_ref[...] = jnp.where(lane == 0, w1, jnp.where(lane == 1, e * w1, 0.0))


def _router(hf_chunks, l, W, tr):
    n = hf_chunks[0].shape[0]
    out = pl.BlockSpec((tr, LANES), lambda i: (i, 0))
    return pl.pallas_call(
        _router_kernel,
        grid=(n // tr,),
        in_specs=[pl.BlockSpec((tr, SC_COLS), lambda i: (i, 0))] * SC_CHUNKS
                 + [_layer_spec(l // 2, (D_MODEL, LANES)), _layer_spec(l // 2, (D_MODEL, LANES))],
        out_specs=[out, out],
        out_shape=[jax.ShapeDtypeStruct((n, LANES), jnp.int32), jax.ShapeDtypeStruct((n, LANES), F32)],
        compiler_params=_params("parallel"),
        name="router",
    )(*hf_chunks, W["w_router_hi"], W["w_router_lo"])


def _route_plan(ridx, tmx):
    n = ridx.shape[0]
    e = ridx[:, :2].T.reshape(2 * n)
    onehot = (e[:, None] == jnp.arange(N_EXPERTS, dtype=jnp.int32)[None, :]).astype(jnp.int32)
    csum = jnp.cumsum(onehot, axis=0)
    rank = jnp.sum((csum - onehot) * onehot, axis=-1)
    counts = csum[-1]
    padded = ((counts + tmx - 1) // tmx) * tmx
    ends = jnp.cumsum(padded)
    starts = ends - padded
    dest = jnp.sum(onehot * starts[None, :], axis=-1) + rank
    n_tiles = (2 * n) // tmx + N_EXPERTS
    tile_row0 = jnp.arange(n_tiles, dtype=jnp.int32) * tmx
    tile_e = jnp.minimum(jnp.sum((tile_row0[:, None] >= ends[None, :]).astype(jnp.int32), axis=-1),
                         N_EXPERTS - 1)
    group_end = jnp.take(starts + counts, tile_e)
    tile_valid = jnp.where(tile_row0 < ends[-1], jnp.clip(group_end - tile_row0, 0, tmx), 0)
    return dest.reshape(2, n).astype(jnp.int32), tile_e.astype(jnp.int32), tile_valid.astype(jnp.int32), n_tiles * tmx


def _sc_mesh():
    return plsc.VectorSubcoreMesh(core_axis_name="core", subcore_axis_name="subcore")


def _sc_scatter_rows(chunks, dest, n_out):
    n, width = chunks[0].shape
    per_choice = n // SC_WINDOW
    out_type = [jax.ShapeDtypeStruct((n_out, width), chunks[0].dtype)] * len(chunks)

    @pl.kernel(out_type=out_type, mesh=_sc_mesh(), scratch_types=[])
    def scatter(*refs):
        x_refs, i_hbm, o_refs = refs[:len(chunks)], refs[len(chunks)], refs[len(chunks) + 1:]
        for x_hbm, o_hbm in zip(x_refs, o_refs):
            def body(x_vmem, i_vmem, o_hbm=o_hbm):
                pltpu.sync_copy(x_vmem, o_hbm.at[i_vmem.at[0]])

            pltpu.emit_pipeline(
                body,
                grid=(2 * per_choice,),
                in_specs=[pl.BlockSpec((SC_WINDOW, width), lambda i: (i % per_choice, 0)),
                          pl.BlockSpec((1, SC_WINDOW), lambda i: (0, i))],
                out_specs=[],
                core_axis_name=("core", "subcore"),
                dimension_semantics=(pltpu.PARALLEL,),
            )(x_hbm, i_hbm)

    return scatter(*chunks, dest.reshape(1, 2 * n))


def _sc_gather_rows(chunks, idx):
    n = idx.shape[0]
    width = chunks[0].shape[1]
    out_type = [jax.ShapeDtypeStruct((n, width), chunks[0].dtype)] * len(chunks)

    @pl.kernel(out_type=out_type, mesh=_sc_mesh(), scratch_types=[])
    def gather(*refs):
        x_refs, i_hbm, o_refs = refs[:len(chunks)], refs[len(chunks)], refs[len(chunks) + 1:]
        for x_hbm, o_hbm in zip(x_refs, o_refs):
            def body(i_vmem, o_vmem, x_hbm=x_hbm):
                pltpu.sync_copy(x_hbm.at[i_vmem.at[0]], o_vmem)

            pltpu.emit_pipeline(
                body,
                grid=(n // SC_WINDOW,),
                in_specs=[pl.BlockSpec((1, SC_WINDOW), lambda i: (0, i))],
                out_specs=[pl.BlockSpec((SC_WINDOW, width), lambda i: (i, 0))],
                core_axis_name=("core", "subcore"),
                dimension_semantics=(pltpu.PARALLEL,),
            )(i_hbm, o_hbm)

    return gather(*chunks, idx.reshape(1, n))


def _moe_group_kernel(te_ref, tv_ref, *refs):
    xs_refs, (wgu_ref, wd_ref) = refs[:SC_CHUNKS], refs[SC_CHUNKS:SC_CHUNKS + 2]
    ys_refs = refs[SC_CHUNKS + 2:]
    valid = tv_ref[pl.program_id(0)]

    @pl.when(valid > 0)
    def _():
        row = lax.broadcasted_iota(jnp.int32, xs_refs[0].shape, 0)
        h = jnp.concatenate([jnp.where(row < valid, xs_ref[...], 0.0).astype(BF16) for xs_ref in xs_refs], axis=-1)
        y = _swiglu_down(h, wgu_ref.at[0], wd_ref.at[0], jnp.zeros((h.shape[0], D_MODEL), F32))
        for c, ys_ref in enumerate(ys_refs):
            ys_ref[...] = y[:, c * SC_COLS:(c + 1) * SC_COLS]


def _moe_group(xs, tile_e, tile_valid, l, W, tmx):
    n_rows = xs[0].shape[0]
    chunk = pl.BlockSpec((tmx, SC_COLS), lambda t, te, tv: (t, 0))
    grid_spec = pltpu.PrefetchScalarGridSpec(
        num_scalar_prefetch=2,
        grid=(n_rows // tmx,),
        in_specs=[chunk] * SC_CHUNKS + [
            pl.BlockSpec((None, 1, D_MODEL, 2 * D_FF), lambda t, te, tv: (l // 2, te[t], 0, 0)),
            pl.BlockSpec((None, 1, D_FF, D_MODEL), lambda t, te, tv: (l // 2, te[t], 0, 0))],
        out_specs=[chunk] * SC_CHUNKS,
    )
    return pl.pallas_call(
        _moe_group_kernel,
        grid_spec=grid_spec,
        out_shape=[jax.ShapeDtypeStruct((n_rows, SC_COLS), F32)] * SC_CHUNKS,
        compiler_params=_params("arbitrary"),
        name="moe_group",
    )(tile_e, tile_valid, *xs, W["w_gu_moe"], W["w_down_moe"])


def _moe_combine_kernel(x_ref, rw_ref, *refs):
    y_refs, o_ref = refs[:-1], refs[-1]
    rw = rw_ref[...]
    w0, w1 = rw[:, 0:1], rw[:, 1:2]
    for c in range(SC_CHUNKS):
        sl = slice(c * SC_COLS, (c + 1) * SC_COLS)
        o_ref[:, sl] = x_ref[:, sl] + w0 * y_refs[2 * c][...] + w1 * y_refs[2 * c + 1][...]


def _moe_combine(x, yg, rw, tm):
    n = x.shape[0]
    per_choice = n // tm
    row = pl.BlockSpec((tm, D_MODEL), lambda i: (i, 0))
    first = pl.BlockSpec((tm, SC_COLS), lambda i: (i, 0))
    second = pl.BlockSpec((tm, SC_COLS), lambda i: (i + per_choice, 0))
    args = []
    for y in yg:
        args += [y, y]
    return pl.pallas_call(
        _moe_combine_kernel,
        grid=(n // tm,),
        in_specs=[row, pl.BlockSpec((tm, LANES), lambda i: (i, 0))] + [first, second] * SC_CHUNKS,
        out_specs=row,
        out_shape=jax.ShapeDtypeStruct((n, D_MODEL), F32),
        compiler_params=_params("parallel"),
        name="moe_combine",
    )(x, rw, *args)


def _moe(hf_chunks, x, l, W, tm):
    n = x.shape[0]
    ridx, rw = _router(hf_chunks, l, W, min(2048, n))
    dest, tile_e, tile_valid, n_rows = _route_plan(ridx, MOE_TILE)
    xs = _sc_scatter_rows(hf_chunks, dest, n_rows)
    ys = _moe_group(xs, tile_e, tile_valid, l, W, MOE_TILE)
    yg = _sc_gather_rows(ys, dest.reshape(2 * n))
    return _moe_combine(x, yg, rw, tm)


def _rope_tables(seq_len):
    half = QK_ROPE // 2
    freqs = jnp.power(ROPE_THETA, -jnp.arange(half, dtype=F32) / half)
    ang = jnp.arange(seq_len).astype(F32)[:, None] * freqs[None, :]
    c, s = jnp.cos(ang), jnp.sin(ang)
    z = lambda n: jnp.zeros((seq_len, n), F32)
    o = lambda n: jnp.ones((seq_len, n), F32)
    tail = LANES - QK_NOPE - QK_ROPE
    cos = jnp.concatenate([o(QK_NOPE), c, c, o(tail)], axis=-1)
    sina = jnp.concatenate([z(QK_NOPE), -s, z(half), z(tail)], axis=-1)
    sinb = jnp.concatenate([z(QK_NOPE), z(half), s, z(tail)], axis=-1)
    return cos, sina, sinb


def _pad_last(a, width):
    return jnp.pad(a, [(0, 0)] * (a.ndim - 1) + [(0, width - a.shape[-1])])


def _prep_weights(g_mix, w_in, pool_w, pool_scale, g_cq, w_uq, g_ckv, w_ukv, g_qn_mla, g_kn_mla, g_mem,
                  w_mem_kv, g_qn_mem, g_kn_mem, w_br, w_out, g_ffn, w_gu_dense, w_down_dense, w_router,
                  w_gu_moe, w_down_moe):
    depth = w_in.shape[0]
    kr_lo = POOL_WIDTH + Q_LORA + KV_LORA
    kr_hi = kr_lo + QK_ROPE
    wi = w_in.astype(BF16)
    zc = lambda n: jnp.zeros((depth, D_MODEL, n), BF16)
    w_in_p = jnp.concatenate([wi[..., :kr_lo], zc(QK_NOPE), wi[..., kr_lo:kr_hi], zc(LANES - QK_NOPE - QK_ROPE),
                              wi[..., kr_hi:]], axis=-1)
    w_uq_p = _pad_last(w_uq.astype(BF16).reshape(depth, Q_LORA, MLA_HEADS, QK_DIM), LANES)
    ukv = w_ukv.astype(BF16).reshape(depth, KV_LORA, MLA_HEADS, QK_NOPE + V_DIM)
    w_uk_p = _pad_last(ukv[..., :QK_NOPE], LANES)
    w_uv_p = _pad_last(ukv[..., QK_NOPE:], LANES)
    row = lambda a: a[:, None, :]
    wr = _pad_last(w_router, LANES)
    wr_hi = wr.astype(BF16)
    return {
        "g_mix": row(g_mix), "w_in": w_in_p,
        "g_cq": row(g_cq), "w_uq": w_uq_p.reshape(depth, Q_LORA, MLA_HEADS * LANES),
        "g_ckv": row(g_ckv), "w_uk": w_uk_p.reshape(depth, KV_LORA, MLA_HEADS * LANES),
        "w_uv": w_uv_p.reshape(depth, KV_LORA, MLA_HEADS * LANES),
        "g_qn_mla": _pad_last(row(g_qn_mla), LANES), "g_kn_mla": _pad_last(row(g_kn_mla), LANES),
        "g_qn_mem": row(g_qn_mem), "g_kn_mem": row(g_kn_mem),
        "g_mem": row(g_mem), "w_mem_kv": w_mem_kv.astype(BF16),
        "pool_w": pool_w.astype(BF16), "pool_scale": row(pool_scale),
        "w_br": w_br.astype(BF16), "w_out": w_out.astype(BF16), "g_ffn": row(g_ffn),
        "w_gu_dense": w_gu_dense.astype(BF16), "w_down_dense": w_down_dense.astype(BF16),
        "w_router_hi": wr_hi, "w_router_lo": (wr - wr_hi.astype(F32)).astype(BF16),
        "w_gu_moe": w_gu_moe.astype(BF16), "w_down_moe": w_down_moe.astype(BF16),
    }


def _trunk(x, mem, W):
    B, S, _ = x.shape
    tm = min(512, S)
    tq = min(1024, S)
    tk = min(max(S // 2, 1024), S)
    tp = min(256, S)
    tn = min(512, B * S)
    tables = _rope_tables(S)
    for l in range(DEPTH):
        up, q, k, v, qm, gate = _mix_in(x, l, W, tables, tm)
        km, vm = _mem_kv(mem, l, W)
        yp = _pool(up, l, W, tp)
        ym = _mla_attn(q, k, v, tq, tk)
        if l % 2 == 0:
            x_mid, hf = _merge(False, yp, ym, qm, km, vm, gate, x, l, W, tm)
            x = _ffn(hf.reshape(B * S, D_MODEL), x_mid.reshape(B * S, D_MODEL), l, W, tn)
        else:
            x_mid, *hf = _merge(True, yp, ym, qm, km, vm, gate, x, l, W, tm)
            x = _moe([h.reshape(B * S, SC_COLS) for h in hf], x_mid.reshape(B * S, D_MODEL), l, W, tn)
        x = x.reshape(B, S, D_MODEL)
    return x


def kernel(x_prompt, x_sample, mem_prompt, mem_sample, g_mix, w_in, pool_w, pool_scale, g_cq, w_uq, g_ckv, w_ukv, g_qn_mla, g_kn_mla, g_mem, w_mem_kv, g_qn_mem, g_kn_mem, w_br, w_out, g_ffn, w_gu_dense, w_down_dense, w_router, w_gu_moe, w_down_moe):
    W = _prep_weights(g_mix, w_in, pool_w, pool_scale, g_cq, w_uq, g_ckv, w_ukv, g_qn_mla, g_kn_mla, g_mem,
                      w_mem_kv, g_qn_mem, g_kn_mem, w_br, w_out, g_ffn, w_gu_dense, w_down_dense, w_router,
                      w_gu_moe, w_down_moe)
    y_prompt = _trunk(x_prompt, mem_prompt, W)
    y_sample = _trunk(x_sample, mem_sample, W)
    return (y_prompt, y_sample)
```

```python
import functools

import numpy as np

import jax
import jax.numpy as jnp
from jax import lax
from jax.experimental import pallas as pl
from jax.experimental.pallas import tpu as pltpu
from jax.experimental.pallas import tpu_sc as plsc

D_MODEL = 1024
DEPTH = 4
POOL_WINDOWS = (2, 4, 8, 16)
N_POOL_GROUPS = 4
POOL_GROUP_DIM = 128
POOL_WIDTH = 512
MLA_HEADS = 8
QK_NOPE = 64
QK_ROPE = 32
QK_DIM = 96
V_DIM = 64
Q_LORA = 384
KV_LORA = 256
ROPE_THETA = 10000.0
MEM_TOKENS = 256
MEM_HEADS = 4
MEM_HEAD_DIM = 128
MEM_WIDTH = 512
N_BRANCH = 3
D_FF = 2816
N_EXPERTS = 8
EPS = 1e-6
LOG2_E = 1.4426950408889634

LANES = 128
POOL_HALO = 16
POOL_ROWS = 256
VMEM_LIMIT_BYTES = 56 * 1024 * 1024
MOE_TILE = 512
FF_CHUNK = 256
SC_WINDOW = 128
SC_COLS = 256
SC_CHUNKS = D_MODEL // SC_COLS

C_POOL = 0
C_CQ = C_POOL + POOL_WIDTH
C_CKV = C_CQ + Q_LORA
C_KR = C_CKV + KV_LORA
C_QM = C_KR + LANES
C_GATE = C_QM + MEM_WIDTH
IN_PACKED = C_GATE + N_BRANCH * D_MODEL

BF16 = jnp.bfloat16
F32 = jnp.float32
NT_DIMS = (((1,), (1,)), ((), ()))


def _const_spec(shape):
    zeros = (0,) * len(shape)
    return pl.BlockSpec(shape, lambda *_: zeros, pipeline_mode=pl.Buffered(1))


def _layer_spec(l, shape):
    zeros = (0,) * len(shape)
    return pl.BlockSpec((None,) + tuple(shape), lambda *_: (l,) + zeros, pipeline_mode=pl.Buffered(1))


def _params(*sem):
    return pltpu.CompilerParams(dimension_semantics=sem, vmem_limit_bytes=VMEM_LIMIT_BYTES)


def _rms(x, width):
    return x * lax.rsqrt(jnp.sum(x * x, axis=-1, keepdims=True) * (1.0 / width) + EPS)


def _dot(a, b):
    return jnp.dot(a, b, preferred_element_type=F32)


def _mix_in_kernel(n_pending, x_ref, *refs):
    pending, refs = refs[:n_pending], refs[n_pending:]
    (gmix_ref, win_ref, gcq_ref, wuq_ref, gckv_ref, wuk_ref, wuv_ref, cos_ref, sina_ref, sinb_ref,
     gqn_ref, gkn_ref, gqm_ref, up_ref, q_ref, k_ref, v_ref, qm_ref, gate_ref) = refs[:19]
    x = x_ref[0]
    if n_pending:
        x = _expert_sum(x, pending[0][0], [y.at[0, 0] for y in pending[1:]])
        refs[19][0] = x
    hn = (_rms(x, D_MODEL) * gmix_ref[...]).astype(BF16)

    def proj(lo, hi):
        return _dot(hn, win_ref[:, lo:hi])

    cos, sina, sinb = cos_ref[...], sina_ref[...], sinb_ref[...]

    def rope(t):
        return t * cos + pltpu.roll(t, LANES - QK_ROPE // 2, 1) * sina + pltpu.roll(t, QK_ROPE // 2, 1) * sinb

    up_ref[0] = proj(C_POOL, C_CQ).astype(BF16)

    cqn = (_rms(proj(C_CQ, C_CKV), Q_LORA) * gcq_ref[...]).astype(BF16)
    qf = _dot(cqn, wuq_ref[...])
    gq = gqn_ref[...] * (QK_DIM ** -0.5 * LOG2_E)
    for h in range(MLA_HEADS):
        qh = rope(qf[:, h * LANES:(h + 1) * LANES])
        q_ref[0, h] = (_rms(qh, QK_DIM) * gq).astype(BF16)

    ckvn = (_rms(proj(C_CKV, C_KR), KV_LORA) * gckv_ref[...]).astype(BF16)
    kf = _dot(ckvn, wuk_ref[...])
    kr = rope(proj(C_KR, C_QM))
    gk = gkn_ref[...]
    for h in range(MLA_HEADS):
        kh = kf[:, h * LANES:(h + 1) * LANES] + kr
        k_ref[0, h] = (_rms(kh, QK_DIM) * gk).astype(BF16)
    vf = _dot(ckvn, wuv_ref[...])
    lane = lax.broadcasted_iota(jnp.int32, (vf.shape[0], LANES), 1)
    for h in range(MLA_HEADS):
        v_ref[0, h] = jnp.where(lane < V_DIM, vf[:, h * LANES:(h + 1) * LANES], 1.0).astype(BF16)

    qm = proj(C_QM, C_GATE)
    gm = gqm_ref[...] * (MEM_HEAD_DIM ** -0.5)
    for h in range(MEM_HEADS):
        qh = qm[:, h * LANES:(h + 1) * LANES]
        qm_ref[0, h] = (_rms(qh, MEM_HEAD_DIM) * gm).astype(BF16)

    for c in range(N_BRANCH):
        gl = proj(C_GATE + c * D_MODEL, C_GATE + (c + 1) * D_MODEL)
        gate_ref[0, :, c * D_MODEL:(c + 1) * D_MODEL] = jax.nn.sigmoid(gl).astype(BF16)


def _mix_in(x, pending, l, W, tables, tm):
    B, S, _ = x.shape
    cos, sina, sinb = tables
    row = lambda w: pl.BlockSpec((1, tm, w), lambda b, i: (b, i, 0))
    heads = lambda n: pl.BlockSpec((1, n, tm, LANES), lambda b, i: (b, 0, i, 0))
    tab = pl.BlockSpec((tm, LANES), lambda b, i: (i, 0))
    ls = functools.partial(_layer_spec, l)
    head_shape = jax.ShapeDtypeStruct((B, MLA_HEADS, S, LANES), BF16)
    pend_args, pend_specs = _expert_args(*pending, B, S, tm) if pending else ([], [])
    out_specs = [row(POOL_WIDTH), heads(MLA_HEADS), heads(MLA_HEADS), heads(MLA_HEADS),
                 heads(MEM_HEADS), row(N_BRANCH * D_MODEL)]
    out_shape = [jax.ShapeDtypeStruct((B, S, POOL_WIDTH), BF16), head_shape, head_shape, head_shape,
                 jax.ShapeDtypeStruct((B, MEM_HEADS, S, LANES), BF16),
                 jax.ShapeDtypeStruct((B, S, N_BRANCH * D_MODEL), BF16)]
    if pending:
        out_specs.append(row(D_MODEL))
        out_shape.append(jax.ShapeDtypeStruct((B, S, D_MODEL), F32))
    return pl.pallas_call(
        functools.partial(_mix_in_kernel, len(pend_args)),
        grid=(B, S // tm),
        in_specs=[row(D_MODEL)] + pend_specs + [
            ls((1, D_MODEL)), ls((D_MODEL, IN_PACKED)),
            ls((1, Q_LORA)), ls((Q_LORA, MLA_HEADS * LANES)),
            ls((1, KV_LORA)), ls((KV_LORA, MLA_HEADS * LANES)), ls((KV_LORA, MLA_HEADS * LANES)),
            tab, tab, tab,
            ls((1, LANES)), ls((1, LANES)), ls((1, LANES))],
        out_specs=out_specs,
        out_shape=out_shape,
        compiler_params=_params("parallel", "parallel"),
        name="mix_in",
    )(x, *pend_args, W["g_mix"], W["w_in"], W["g_cq"], W["w_uq"], W["g_ckv"], W["w_uk"], W["w_uv"],
      cos, sina, sinb, W["g_qn_mla"], W["g_kn_mla"], W["g_qn_mem"])


def _mem_kv_kernel(mem_ref, gmem_ref, w_ref, gkn_ref, km_ref, vm_ref):
    hn = (_rms(mem_ref[0], D_MODEL) * gmem_ref[...]).astype(BF16)
    kv = _dot(hn, w_ref[...])
    for h in range(MEM_HEADS):
        kh = kv[:, h * LANES:(h + 1) * LANES]
        km_ref[0, h] = (_rms(kh, MEM_HEAD_DIM) * gkn_ref[...]).astype(BF16)
        vm_ref[0, h] = kv[:, MEM_WIDTH + h * LANES:MEM_WIDTH + (h + 1) * LANES].astype(BF16)


def _mem_kv(mem, l, W):
    B, M, _ = mem.shape
    out = pl.BlockSpec((1, MEM_HEADS, M, LANES), lambda b: (b, 0, 0, 0))
    shape = jax.ShapeDtypeStruct((B, MEM_HEADS, M, LANES), BF16)
    return pl.pallas_call(
        _mem_kv_kernel,
        grid=(B,),
        in_specs=[pl.BlockSpec((1, M, D_MODEL), lambda b: (b, 0, 0)), _layer_spec(l, (1, D_MODEL)),
                  _layer_spec(l, (D_MODEL, 2 * MEM_WIDTH)), _layer_spec(l, (1, LANES))],
        out_specs=[out, out],
        out_shape=[shape, shape],
        compiler_params=_params("parallel"),
        name="mem_kv",
    )(mem, W["g_mem"], W["w_mem_kv"], W["g_kn_mem"])


def _pool_bands():
    i = np.arange(POOL_ROWS)[:, None]
    r = np.arange(-POOL_HALO, POOL_ROWS + POOL_HALO)[None, :]
    bands = [((r - i >= -(w // 2)) & (r - i < w // 2)).astype(np.float32) for w in POOL_WINDOWS]
    return jnp.asarray(np.stack(bands), BF16)


def _pool_mix(seq_len, row0, has_prev, has_next, cur_ref, prev_ref, next_ref, band_ref, pw_ref, ps_ref):
    tm = cur_ref.shape[1]
    prev = jnp.where(has_prev, prev_ref[0], jnp.zeros_like(prev_ref[0]))
    nxt = jnp.where(has_next, next_ref[0], jnp.zeros_like(next_ref[0]))
    ext = jnp.concatenate([prev, cur_ref[0], nxt], axis=0)
    out = []
    for r in range(0, tm, POOL_ROWS):
        win = ext[r:r + POOL_ROWS + 2 * POOL_HALO]
        t = row0 + r + lax.broadcasted_iota(jnp.int32, (POOL_ROWS, 1), 0)
        groups = []
        for g, w in enumerate(POOL_WINDOWS):
            sl = slice(g * POOL_GROUP_DIM, (g + 1) * POOL_GROUP_DIM)
            count = jnp.minimum(t + w // 2, seq_len) - jnp.maximum(t - w // 2, 0)
            u = win[POOL_HALO:POOL_HALO + POOL_ROWS, sl].astype(F32)
            d = _dot(band_ref[g], win[:, sl]) / count.astype(F32) - u
            groups.append(_dot(d.astype(BF16), pw_ref[g]) * ps_ref[:, sl])
        out.append(jnp.concatenate(groups, axis=-1).astype(BF16))
    return jnp.concatenate(out, axis=0)


def _softmax_pv(s, v):
    m = jnp.max(s, axis=-1, keepdims=True)
    p = jnp.exp(s - m)
    l = jnp.sum(p, axis=-1, keepdims=True)
    return _dot(p.astype(BF16), v) / l


def _mla_attn_kernel(seq_len, tk, q_ref, k_ref, v_ref, o_ref):
    outs = []
    for hh in range(2):
        q = q_ref[0, hh]
        m = acc = None
        for c in range(seq_len // tk):
            ks = slice(c * tk, (c + 1) * tk)
            s = lax.dot_general(q, k_ref[0, hh, ks, :], NT_DIMS, preferred_element_type=F32)
            mc = jnp.max(s, axis=-1, keepdims=True)
            m_new = mc if m is None else jnp.maximum(m, mc)
            pv = _dot(jnp.exp2(s - m_new).astype(BF16), v_ref[0, hh, ks, :])
            acc = pv if acc is None else acc * jnp.exp2(m - m_new) + pv
            m = m_new
        outs.append(acc * (1.0 / acc[:, V_DIM:V_DIM + 1]))
    lane = lax.broadcasted_iota(jnp.int32, outs[0].shape, 1)
    o_ref[0] = jnp.where(lane < V_DIM, outs[0], pltpu.roll(outs[1], V_DIM, 1)).astype(BF16)


def _mla_attn(q, k, v, tq, tk):
    B, H, S, _ = q.shape
    kv = pl.BlockSpec((1, 2, S, LANES), lambda b, p, i: (b, p, 0, 0))
    return pl.pallas_call(
        functools.partial(_mla_attn_kernel, S, tk),
        grid=(B, H // 2, S // tq),
        in_specs=[pl.BlockSpec((1, 2, tq, LANES), lambda b, p, i: (b, p, i, 0)), kv, kv],
        out_specs=pl.BlockSpec((1, tq, 2 * V_DIM), lambda b, p, i: (b, i, p)),
        out_shape=jax.ShapeDtypeStruct((B, S, H * V_DIM), BF16),
        compiler_params=_params("parallel", "parallel", "arbitrary"),
        name="mla_attn",
    )(q, k, v)


def _merge_kernel(is_moe, seq_len, up_ref, prev_ref, next_ref, band_ref, pw_ref, ps_ref,
                  ym_ref, qm_ref, km_ref, vm_ref, gate_ref, wbr_ref, wout_ref,
                  x_ref, gffn_ref, xo_ref, *hf_refs):
    i = pl.program_id(1)
    y_pool = _pool_mix(seq_len, i * up_ref.shape[1], i > 0, i < pl.num_programs(1) - 1,
                       up_ref, prev_ref, next_ref, band_ref, pw_ref, ps_ref)

    mem = []
    for h in range(MEM_HEADS):
        s = lax.dot_general(qm_ref[0, h], km_ref[0, h], NT_DIMS, preferred_element_type=F32)
        mem.append(_softmax_pv(s, vm_ref[0, h]).astype(BF16))
    y_mem = jnp.concatenate(mem, axis=-1)

    branches = (y_pool, ym_ref[0], y_mem)
    merged = None
    for c, y in enumerate(branches):
        term = gate_ref[0, :, c * D_MODEL:(c + 1) * D_MODEL].astype(F32) * _dot(y, wbr_ref[c])
        merged = term if merged is None else merged + term
    x_new = x_ref[0] + _dot(merged.astype(BF16), wout_ref[...])
    xo_ref[0] = x_new
    hf = _rms(x_new, D_MODEL) * gffn_ref[...]
    if is_moe:
        for c, ref in enumerate(hf_refs):
            ref[0] = hf[:, c * SC_COLS:(c + 1) * SC_COLS]
    else:
        hf_refs[0][0] = hf.astype(BF16)


def _merge(is_moe, up, ym, qm, km, vm, gate, x, l, W, tm):
    B, S, _ = x.shape
    M = km.shape[2]
    row = lambda w: pl.BlockSpec((1, tm, w), lambda b, i: (b, i, 0))
    memkv = pl.BlockSpec((1, MEM_HEADS, M, LANES), lambda b, i: (b, 0, 0, 0))
    per = tm // POOL_HALO
    last = S // POOL_HALO - 1
    halo = lambda index: pl.BlockSpec((1, POOL_HALO, POOL_WIDTH), lambda b, i: (b, index(i), 0))
    in_specs = [row(POOL_WIDTH), halo(lambda i: jnp.maximum(i * per - 1, 0)),
                halo(lambda i: jnp.minimum((i + 1) * per, last)),
                _const_spec((N_POOL_GROUPS, POOL_ROWS, POOL_ROWS + 2 * POOL_HALO)),
                _layer_spec(l, (N_POOL_GROUPS, POOL_GROUP_DIM, POOL_GROUP_DIM)), _layer_spec(l, (1, POOL_WIDTH)),
                row(MLA_HEADS * V_DIM),
                pl.BlockSpec((1, MEM_HEADS, tm, LANES), lambda b, i: (b, 0, i, 0)), memkv, memkv,
                row(N_BRANCH * D_MODEL), _layer_spec(l, (N_BRANCH, POOL_WIDTH, D_MODEL)),
                _layer_spec(l, (D_MODEL, D_MODEL)), row(D_MODEL), _layer_spec(l, (1, D_MODEL))]
    out_specs = [row(D_MODEL)]
    out_shape = [jax.ShapeDtypeStruct((B, S, D_MODEL), F32)]
    if is_moe:
        out_specs += [row(SC_COLS)] * SC_CHUNKS
        out_shape += [jax.ShapeDtypeStruct((B, S, SC_COLS), F32)] * SC_CHUNKS
    else:
        out_specs.append(row(D_MODEL))
        out_shape.append(jax.ShapeDtypeStruct((B, S, D_MODEL), BF16))
    return pl.pallas_call(
        functools.partial(_merge_kernel, is_moe, S),
        grid=(B, S // tm),
        in_specs=in_specs, out_specs=out_specs, out_shape=out_shape,
        compiler_params=_params("parallel", "parallel"),
        name="merge_moe" if is_moe else "merge",
    )(up, up, up, _pool_bands(), W["pool_w"], W["pool_scale"],
      ym, qm, km, vm, gate, W["w_br"], W["w_out"], x, W["g_ffn"])


def _swiglu_chunk(h, wg, wu):
    g = _dot(h, wg)
    return g * jax.nn.sigmoid(g) * _dot(h, wu)


def _swiglu_down(h, wgu_ref, wd_ref, acc):
    for j in range(D_FF // FF_CHUNK):
        lo, hi = j * FF_CHUNK, (j + 1) * FF_CHUNK
        a = _swiglu_chunk(h, wgu_ref[:, lo:hi], wgu_ref[:, D_FF + lo:D_FF + hi])
        acc = acc + _dot(a.astype(BF16), wd_ref[lo:hi, :])
    return acc


def _ffn_kernel(hf_ref, wgu_ref, wd_ref, x_ref, o_ref):
    o_ref[...] = _swiglu_down(hf_ref[...], wgu_ref, wd_ref, x_ref[...])


def _ffn(hf, x, l, W, tm):
    N = hf.shape[0]
    row = pl.BlockSpec((tm, D_MODEL), lambda i: (i, 0))
    return pl.pallas_call(
        _ffn_kernel,
        grid=(N // tm,),
        in_specs=[row, _layer_spec(l // 2, (D_MODEL, 2 * D_FF)), _layer_spec(l // 2, (D_FF, D_MODEL)), row],
        out_specs=row,
        out_shape=jax.ShapeDtypeStruct((N, D_MODEL), F32),
        compiler_params=_params("parallel"),
        name="ffn",
    )(hf, W["w_gu_dense"], W["w_down_dense"], x)


def _router_kernel(*refs):
    hf_refs, (wrh_ref, wrl_ref, ridx_ref, rw_ref) = refs[:SC_CHUNKS], refs[SC_CHUNKS:]
    logits = None
    for c, hf_ref in enumerate(hf_refs):
        rows = slice(c * SC_COLS, (c + 1) * SC_COLS)
        hf = hf_ref[...]
        hi = hf.astype(BF16)
        lo = (hf - hi.astype(F32)).astype(BF16)
        part = _dot(hi, wrh_ref[rows, :]) + _dot(lo, wrh_ref[rows, :]) + _dot(hi, wrl_ref[rows, :])
        logits = part if logits is None else logits + part
    lane = lax.broadcasted_iota(jnp.int32, logits.shape, 1)
    lg = jnp.where(lane < N_EXPERTS, logits, -jnp.inf)
    m1 = jnp.max(lg, axis=-1, keepdims=True)
    i1 = jnp.min(jnp.where(lg == m1, lane, LANES), axis=-1, keepdims=True)
    lg2 = jnp.where(lane == i1, -jnp.inf, lg)
    m2 = jnp.max(lg2, axis=-1, keepdims=True)
    i2 = jnp.min(jnp.where(lg2 == m2, lane, LANES), axis=-1, keepdims=True)
    e = jnp.exp(m2 - m1)
    w1 = 1.0 / (1.0 + e)
    ridx_ref[...] = jnp.where(lane == 0, i1, jnp.where(lane == 1, i2, 0))
    rw_ref[...] = jnp.where(lane == 0, w1, jnp.where(lane == 1, e * w1, 0.0))


def _router(hf_chunks, l, W, tr):
    n = hf_chunks[0].shape[0]
    out = pl.BlockSpec((tr, LANES), lambda i: (i, 0))
    return pl.pallas_call(
        _router_kernel,
        grid=(n // tr,),
        in_specs=[pl.BlockSpec((tr, SC_COLS), lambda i: (i, 0))] * SC_CHUNKS
                 + [_layer_spec(l // 2, (D_MODEL, LANES)), _layer_spec(l // 2, (D_MODEL, LANES))],
        out_specs=[out, out],
        out_shape=[jax.ShapeDtypeStruct((n, LANES), jnp.int32), jax.ShapeDtypeStruct((n, LANES), F32)],
        compiler_params=_params("parallel"),
        name="router",
    )(*hf_chunks, W["w_router_hi"], W["w_router_lo"])


def _route_plan(ridx, tmx):
    n = ridx.shape[0]
    e = ridx[:, :2].T.reshape(2 * n)
    onehot = (e[:, None] == jnp.arange(N_EXPERTS, dtype=jnp.int32)[None, :]).astype(jnp.int32)
    csum = jnp.cumsum(onehot, axis=0)
    rank = jnp.sum((csum - onehot) * onehot, axis=-1)
    counts = csum[-1]
    padded = ((counts + tmx - 1) // tmx) * tmx
    ends = jnp.cumsum(padded)
    starts = ends - padded
    dest = jnp.sum(onehot * starts[None, :], axis=-1) + rank
    n_tiles = (2 * n) // tmx + N_EXPERTS
    tile_row0 = jnp.arange(n_tiles, dtype=jnp.int32) * tmx
    tile_e = jnp.minimum(jnp.sum((tile_row0[:, None] >= ends[None, :]).astype(jnp.int32), axis=-1),
                         N_EXPERTS - 1)
    group_end = jnp.take(starts + counts, tile_e)
    tile_valid = jnp.where(tile_row0 < ends[-1], jnp.clip(group_end - tile_row0, 0, tmx), 0)
    return dest.reshape(2, n).astype(jnp.int32), tile_e.astype(jnp.int32), tile_valid.astype(jnp.int32), n_tiles * tmx


def _sc_mesh():
    return plsc.VectorSubcoreMesh(core_axis_name="core", subcore_axis_name="subcore")


def _sc_scatter_rows(chunks, dest, n_out):
    n, width = chunks[0].shape
    per_choice = n // SC_WINDOW
    out_type = [jax.ShapeDtypeStruct((n_out, width), chunks[0].dtype)] * len(chunks)

    @pl.kernel(out_type=out_type, mesh=_sc_mesh(), scratch_types=[])
    def scatter(*refs):
        x_refs, i_hbm, o_refs = refs[:len(chunks)], refs[len(chunks)], refs[len(chunks) + 1:]
        for x_hbm, o_hbm in zip(x_refs, o_refs):
            def body(x_vmem, i_vmem, o_hbm=o_hbm):
                pltpu.sync_copy(x_vmem, o_hbm.at[i_vmem.at[0]])

            pltpu.emit_pipeline(
                body,
                grid=(2 * per_choice,),
                in_specs=[pl.BlockSpec((SC_WINDOW, width), lambda i: (i % per_choice, 0)),
                          pl.BlockSpec((1, SC_WINDOW), lambda i: (0, i))],
                out_specs=[],
                core_axis_name=("core", "subcore"),
                dimension_semantics=(pltpu.PARALLEL,),
            )(x_hbm, i_hbm)

    return scatter(*chunks, dest.reshape(1, 2 * n))


def _sc_gather_rows(chunks, idx):
    n = idx.shape[0]
    width = chunks[0].shape[1]
    out_type = [jax.ShapeDtypeStruct((n, width), chunks[0].dtype)] * len(chunks)

    @pl.kernel(out_type=out_type, mesh=_sc_mesh(), scratch_types=[])
    def gather(*refs):
        x_refs, i_hbm, o_refs = refs[:len(chunks)], refs[len(chunks)], refs[len(chunks) + 1:]
        for x_hbm, o_hbm in zip(x_refs, o_refs):
            def body(i_vmem, o_vmem, x_hbm=x_hbm):
                pltpu.sync_copy(x_hbm.at[i_vmem.at[0]], o_vmem)

            pltpu.emit_pipeline(
                body,
                grid=(n // SC_WINDOW,),
                in_specs=[pl.BlockSpec((1, SC_WINDOW), lambda i: (0, i))],
                out_specs=[pl.BlockSpec((SC_WINDOW, width), lambda i: (i, 0))],
                core_axis_name=("core", "subcore"),
                dimension_semantics=(pltpu.PARALLEL,),
            )(i_hbm, o_hbm)

    return gather(*chunks, idx.reshape(1, n))


def _pack_bf16_pair(lo, hi):
    bits = lambda a: lax.bitcast_convert_type(a.astype(BF16).astype(F32), jnp.uint32)
    return (bits(lo) >> 16) | (bits(hi) & jnp.uint32(0xFFFF0000))


def _unpack_bf16_pair(packed):
    f32 = lambda a: lax.bitcast_convert_type(a, F32)
    return f32(packed << 16), f32(packed & jnp.uint32(0xFFFF0000))


def _expert_sum(x, rw, y_refs):
    w0, w1 = rw[:, 0:1], rw[:, 1:2]
    cols = []
    for c in range(SC_CHUNKS // 2):
        lo0, hi0 = _unpack_bf16_pair(y_refs[2 * c][...])
        lo1, hi1 = _unpack_bf16_pair(y_refs[2 * c + 1][...])
        cols += [w0 * lo0 + w1 * lo1, w0 * hi0 + w1 * hi1]
    return x + jnp.concatenate(cols, axis=-1)


def _moe_group_kernel(te_ref, tv_ref, *refs):
    xs_refs, (wgu_ref, wd_ref) = refs[:SC_CHUNKS], refs[SC_CHUNKS:SC_CHUNKS + 2]
    ys_refs = refs[SC_CHUNKS + 2:]
    valid = tv_ref[pl.program_id(0)]

    @pl.when(valid > 0)
    def _():
        row = lax.broadcasted_iota(jnp.int32, xs_refs[0].shape, 0)
        h = jnp.concatenate([jnp.where(row < valid, xs_ref[...], 0.0).astype(BF16) for xs_ref in xs_refs], axis=-1)
        y = _swiglu_down(h, wgu_ref.at[0], wd_ref.at[0], jnp.zeros((h.shape[0], D_MODEL), F32))
        for c, ys_ref in enumerate(ys_refs):
            lo = y[:, 2 * c * SC_COLS:(2 * c + 1) * SC_COLS]
            hi = y[:, (2 * c + 1) * SC_COLS:(2 * c + 2) * SC_COLS]
            ys_ref[...] = _pack_bf16_pair(lo, hi)


def _moe_group(xs, tile_e, tile_valid, l, W, tmx):
    n_rows = xs[0].shape[0]
    chunk = pl.BlockSpec((tmx, SC_COLS), lambda t, te, tv: (t, 0))
    grid_spec = pltpu.PrefetchScalarGridSpec(
        num_scalar_prefetch=2,
        grid=(n_rows // tmx,),
        in_specs=[chunk] * SC_CHUNKS + [
            pl.BlockSpec((None, 1, D_MODEL, 2 * D_FF), lambda t, te, tv: (l // 2, te[t], 0, 0)),
            pl.BlockSpec((None, 1, D_FF, D_MODEL), lambda t, te, tv: (l // 2, te[t], 0, 0))],
        out_specs=[chunk] * (SC_CHUNKS // 2),
    )
    return pl.pallas_call(
        _moe_group_kernel,
        grid_spec=grid_spec,
        out_shape=[jax.ShapeDtypeStruct((n_rows, SC_COLS), jnp.uint32)] * (SC_CHUNKS // 2),
        compiler_params=_params("arbitrary"),
        name="moe_group",
    )(tile_e, tile_valid, *xs, W["w_gu_moe"], W["w_down_moe"])


def _moe_combine_kernel(x_ref, rw_ref, *refs):
    y_refs, o_ref = refs[:-1], refs[-1]
    o_ref[0] = _expert_sum(x_ref[0], rw_ref[0], [y.at[0, 0] for y in y_refs])


def _expert_args(yg, rw, batch, seq_len, tm):
    args = [rw.reshape(batch, seq_len, LANES)]
    specs = [pl.BlockSpec((1, tm, LANES), lambda b, i: (b, i, 0))]
    for y in yg:
        y4 = y.reshape(2, batch, seq_len, SC_COLS)
        for choice in range(2):
            args.append(y4)
            specs.append(pl.BlockSpec((1, 1, tm, SC_COLS), lambda b, i, choice=choice: (choice, b, i, 0)))
    return args, specs


def _moe_combine(x, yg, rw, tm):
    B, S, _ = x.shape
    row = pl.BlockSpec((1, tm, D_MODEL), lambda b, i: (b, i, 0))
    args, specs = _expert_args(yg, rw, B, S, tm)
    return pl.pallas_call(
        _moe_combine_kernel,
        grid=(B, S // tm),
        in_specs=[row] + specs,
        out_specs=row,
        out_shape=jax.ShapeDtypeStruct((B, S, D_MODEL), F32),
        compiler_params=_params("parallel", "parallel"),
        name="moe_combine",
    )(x, *args)


def _moe(hf_chunks, l, W):
    n = hf_chunks[0].shape[0]
    ridx, rw = _router(hf_chunks, l, W, min(2048, n))
    dest, tile_e, tile_valid, n_rows = _route_plan(ridx, MOE_TILE)
    xs = _sc_scatter_rows(hf_chunks, dest, n_rows)
    ys = _moe_group(xs, tile_e, tile_valid, l, W, MOE_TILE)
    yg = _sc_gather_rows(ys, dest.reshape(2 * n))
    return yg, rw


def _rope_tables(seq_len):
    half = QK_ROPE // 2
    freqs = jnp.power(ROPE_THETA, -jnp.arange(half, dtype=F32) / half)
    ang = jnp.arange(seq_len).astype(F32)[:, None] * freqs[None, :]
    c, s = jnp.cos(ang), jnp.sin(ang)
    z = lambda n: jnp.zeros((seq_len, n), F32)
    o = lambda n: jnp.ones((seq_len, n), F32)
    tail = LANES - QK_NOPE - QK_ROPE
    cos = jnp.concatenate([o(QK_NOPE), c, c, o(tail)], axis=-1)
    sina = jnp.concatenate([z(QK_NOPE), -s, z(half), z(tail)], axis=-1)
    sinb = jnp.concatenate([z(QK_NOPE), z(half), s, z(tail)], axis=-1)
    return cos, sina, sinb


def _pad_last(a, width):
    return jnp.pad(a, [(0, 0)] * (a.ndim - 1) + [(0, width - a.shape[-1])])


def _prep_weights(g_mix, w_in, pool_w, pool_scale, g_cq, w_uq, g_ckv, w_ukv, g_qn_mla, g_kn_mla, g_mem,
                  w_mem_kv, g_qn_mem, g_kn_mem, w_br, w_out, g_ffn, w_gu_dense, w_down_dense, w_router,
                  w_gu_moe, w_down_moe):
    depth = w_in.shape[0]
    kr_lo = POOL_WIDTH + Q_LORA + KV_LORA
    kr_hi = kr_lo + QK_ROPE
    wi = w_in.astype(BF16)
    zc = lambda n: jnp.zeros((depth, D_MODEL, n), BF16)
    w_in_p = jnp.concatenate([wi[..., :kr_lo], zc(QK_NOPE), wi[..., kr_lo:kr_hi], zc(LANES - QK_NOPE - QK_ROPE),
                              wi[..., kr_hi:]], axis=-1)
    w_uq_p = _pad_last(w_uq.astype(BF16).reshape(depth, Q_LORA, MLA_HEADS, QK_DIM), LANES)
    ukv = w_ukv.astype(BF16).reshape(depth, KV_LORA, MLA_HEADS, QK_NOPE + V_DIM)
    w_uk_p = _pad_last(ukv[..., :QK_NOPE], LANES)
    w_uv_p = _pad_last(ukv[..., QK_NOPE:], LANES)
    row = lambda a: a[:, None, :]
    wr = _pad_last(w_router, LANES)
    wr_hi = wr.astype(BF16)
    return {
        "g_mix": row(g_mix), "w_in": w_in_p,
        "g_cq": row(g_cq), "w_uq": w_uq_p.reshape(depth, Q_LORA, MLA_HEADS * LANES),
        "g_ckv": row(g_ckv), "w_uk": w_uk_p.reshape(depth, KV_LORA, MLA_HEADS * LANES),
        "w_uv": w_uv_p.reshape(depth, KV_LORA, MLA_HEADS * LANES),
        "g_qn_mla": _pad_last(row(g_qn_mla), LANES), "g_kn_mla": _pad_last(row(g_kn_mla), LANES),
        "g_qn_mem": row(g_qn_mem), "g_kn_mem": row(g_kn_mem),
        "g_mem": row(g_mem), "w_mem_kv": w_mem_kv.astype(BF16),
        "pool_w": pool_w.astype(BF16), "pool_scale": row(pool_scale),
        "w_br": w_br.astype(BF16), "w_out": w_out.astype(BF16), "g_ffn": row(g_ffn),
        "w_gu_dense": w_gu_dense.astype(BF16), "w_down_dense": w_down_dense.astype(BF16),
        "w_router_hi": wr_hi, "w_router_lo": (wr - wr_hi.astype(F32)).astype(BF16),
        "w_gu_moe": w_gu_moe.astype(BF16), "w_down_moe": w_down_moe.astype(BF16),
    }


def _trunk(x, mem, W):
    B, S, _ = x.shape
    tm = min(512, S)
    tq = min(1024, S)
    tk = min(max(S // 2, 1024), S)
    tn = min(512, B * S)
    tables = _rope_tables(S)
    pending = None
    for l in range(DEPTH):
        if pending:
            up, q, k, v, qm, gate, x = _mix_in(x, pending, l, W, tables, tm)
        else:
            up, q, k, v, qm, gate = _mix_in(x, None, l, W, tables, tm)
        km, vm = _mem_kv(mem, l, W)
        ym = _mla_attn(q, k, v, tq, tk)
        if l % 2 == 0:
            x_mid, hf = _merge(False, up, ym, qm, km, vm, gate, x, l, W, tm)
            x = _ffn(hf.reshape(B * S, D_MODEL), x_mid.reshape(B * S, D_MODEL), l, W, tn).reshape(B, S, D_MODEL)
            pending = None
        else:
            x, *hf = _merge(True, up, ym, qm, km, vm, gate, x, l, W, tm)
            pending = _moe([h.reshape(B * S, SC_COLS) for h in hf], l, W)
    if pending:
        x = _moe_combine(x, *pending, tm)
    return x


def kernel(x_prompt, x_sample, mem_prompt, mem_sample, g_mix, w_in, pool_w, pool_scale, g_cq, w_uq, g_ckv, w_ukv, g_qn_mla, g_kn_mla, g_mem, w_mem_kv, g_qn_mem, g_kn_mem, w_br, w_out, g_ffn, w_gu_dense, w_down_dense, w_router, w_gu_moe, w_down_moe):
    W = _prep_weights(g_mix, w_in, pool_w, pool_scale, g_cq, w_uq, g_ckv, w_ukv, g_qn_mla, g_kn_mla, g_mem,
                      w_mem_kv, g_qn_mem, g_kn_mem, w_br, w_out, g_ffn, w_gu_dense, w_down_dense, w_router,
                      w_gu_moe, w_down_moe)
    y_prompt = _trunk(x_prompt, mem_prompt, W)
    y_sample = _trunk(x_sample, mem_sample, W)
    return (y_prompt, y_sample)
```

```python
import functools

import numpy as np

import jax
import jax.numpy as jnp
from jax import lax
from jax.experimental import pallas as pl
from jax.experimental.pallas import tpu as pltpu
from jax.experimental.pallas import tpu_sc as plsc

D_MODEL = 1024
DEPTH = 4
POOL_WINDOWS = (2, 4, 8, 16)
N_POOL_GROUPS = 4
POOL_GROUP_DIM = 128
POOL_WIDTH = 512
MLA_HEADS = 8
QK_NOPE = 64
QK_ROPE = 32
QK_DIM = 96
V_DIM = 64
Q_LORA = 384
KV_LORA = 256
ROPE_THETA = 10000.0
MEM_TOKENS = 256
MEM_HEADS = 4
MEM_HEAD_DIM = 128
MEM_WIDTH = 512
N_BRANCH = 3
D_FF = 2816
N_EXPERTS = 8
EPS = 1e-6
LOG2_E = 1.4426950408889634

LANES = 128
POOL_HALO = 16
POOL_ROWS = 256
VMEM_LIMIT_BYTES = 56 * 1024 * 1024
MOE_TILE = 512
FF_CHUNK = 256
SC_WINDOW = 128
SC_COLS = 256
SC_CHUNKS = D_MODEL // SC_COLS
CAST_BLOCK_BYTES = 8 * 1024 * 1024

C_POOL = 0
C_CQ = C_POOL + POOL_WIDTH
C_CKV = C_CQ + Q_LORA
C_KR = C_CKV + KV_LORA
C_QM = C_KR + LANES
C_GATE = C_QM + MEM_WIDTH
IN_PACKED = C_GATE + N_BRANCH * D_MODEL

BF16 = jnp.bfloat16
F32 = jnp.float32
NT_DIMS = (((1,), (1,)), ((), ()))


def _const_spec(shape):
    zeros = (0,) * len(shape)
    return pl.BlockSpec(shape, lambda *_: zeros, pipeline_mode=pl.Buffered(1))


def _layer_spec(l, shape):
    zeros = (0,) * len(shape)
    return pl.BlockSpec((None,) + tuple(shape), lambda *_: (l,) + zeros, pipeline_mode=pl.Buffered(1))


def _params(*sem):
    return pltpu.CompilerParams(dimension_semantics=sem, vmem_limit_bytes=VMEM_LIMIT_BYTES)


def _rms(x, width):
    return x * lax.rsqrt(jnp.sum(x * x, axis=-1, keepdims=True) * (1.0 / width) + EPS)


def _dot(a, b):
    return jnp.dot(a, b, preferred_element_type=F32)


def _mix_in_kernel(n_pending, x_ref, *refs):
    pending, refs = refs[:n_pending], refs[n_pending:]
    (gmix_ref, win_ref, gcq_ref, wuq_ref, gckv_ref, wuk_ref, wuv_ref, cos_ref, sina_ref, sinb_ref,
     gqn_ref, gkn_ref, gqm_ref, up_ref, q_ref, k_ref, v_ref, qm_ref, gate_ref) = refs[:19]
    x = x_ref[0]
    if n_pending:
        x = _expert_sum(x, pending[0][0], [y.at[0, 0] for y in pending[1:]])
        refs[19][0] = x
    hn = (_rms(x, D_MODEL) * gmix_ref[...]).astype(BF16)

    def proj(lo, hi):
        return _dot(hn, win_ref[:, lo:hi])

    cos, sina, sinb = cos_ref[...], sina_ref[...], sinb_ref[...]

    def rope(t):
        return t * cos + pltpu.roll(t, LANES - QK_ROPE // 2, 1) * sina + pltpu.roll(t, QK_ROPE // 2, 1) * sinb

    up_ref[0] = proj(C_POOL, C_CQ).astype(BF16)

    cqn = (_rms(proj(C_CQ, C_CKV), Q_LORA) * gcq_ref[...]).astype(BF16)
    qf = _dot(cqn, wuq_ref[...])
    gq = gqn_ref[...] * (QK_DIM ** -0.5 * LOG2_E)
    for h in range(MLA_HEADS):
        qh = rope(qf[:, h * LANES:(h + 1) * LANES])
        q_ref[0, h] = (_rms(qh, QK_DIM) * gq).astype(BF16)

    ckvn = (_rms(proj(C_CKV, C_KR), KV_LORA) * gckv_ref[...]).astype(BF16)
    kf = _dot(ckvn, wuk_ref[...])
    kr = rope(proj(C_KR, C_QM))
    gk = gkn_ref[...]
    for h in range(MLA_HEADS):
        kh = kf[:, h * LANES:(h + 1) * LANES] + kr
        k_ref[0, h] = (_rms(kh, QK_DIM) * gk).astype(BF16)
    vf = _dot(ckvn, wuv_ref[...])
    lane = lax.broadcasted_iota(jnp.int32, (vf.shape[0], LANES), 1)
    for h in range(MLA_HEADS):
        v_ref[0, h] = jnp.where(lane < V_DIM, vf[:, h * LANES:(h + 1) * LANES], 1.0).astype(BF16)

    qm = proj(C_QM, C_GATE)
    gm = gqm_ref[...] * (MEM_HEAD_DIM ** -0.5)
    for h in range(MEM_HEADS):
        qh = qm[:, h * LANES:(h + 1) * LANES]
        qm_ref[0, h] = (_rms(qh, MEM_HEAD_DIM) * gm).astype(BF16)

    for c in range(N_BRANCH):
        gl = proj(C_GATE + c * D_MODEL, C_GATE + (c + 1) * D_MODEL)
        gate_ref[0, :, c * D_MODEL:(c + 1) * D_MODEL] = (0.5 * jnp.tanh(0.5 * gl) + 0.5).astype(BF16)


def _mix_in(x, pending, l, W, tables, tm):
    B, S, _ = x.shape
    cos, sina, sinb = tables
    row = lambda w: pl.BlockSpec((1, tm, w), lambda b, i: (b, i, 0))
    heads = lambda n: pl.BlockSpec((1, n, tm, LANES), lambda b, i: (b, 0, i, 0))
    tab = pl.BlockSpec((tm, LANES), lambda b, i: (i, 0))
    ls = functools.partial(_layer_spec, l)
    head_shape = jax.ShapeDtypeStruct((B, MLA_HEADS, S, LANES), BF16)
    pend_args, pend_specs = _expert_args(*pending, B, S, tm) if pending else ([], [])
    out_specs = [row(POOL_WIDTH), heads(MLA_HEADS), heads(MLA_HEADS), heads(MLA_HEADS),
                 heads(MEM_HEADS), row(N_BRANCH * D_MODEL)]
    out_shape = [jax.ShapeDtypeStruct((B, S, POOL_WIDTH), BF16), head_shape, head_shape, head_shape,
                 jax.ShapeDtypeStruct((B, MEM_HEADS, S, LANES), BF16),
                 jax.ShapeDtypeStruct((B, S, N_BRANCH * D_MODEL), BF16)]
    if pending:
        out_specs.append(row(D_MODEL))
        out_shape.append(jax.ShapeDtypeStruct((B, S, D_MODEL), F32))
    return pl.pallas_call(
        functools.partial(_mix_in_kernel, len(pend_args)),
        grid=(B, S // tm),
        in_specs=[row(D_MODEL)] + pend_specs + [
            ls((1, D_MODEL)), ls((D_MODEL, IN_PACKED)),
            ls((1, Q_LORA)), ls((Q_LORA, MLA_HEADS * LANES)),
            ls((1, KV_LORA)), ls((KV_LORA, MLA_HEADS * LANES)), ls((KV_LORA, MLA_HEADS * LANES)),
            tab, tab, tab,
            ls((1, LANES)), ls((1, LANES)), ls((1, LANES))],
        out_specs=out_specs,
        out_shape=out_shape,
        compiler_params=_params("parallel", "parallel"),
        name="mix_in",
    )(x, *pend_args, W["g_mix"], W["w_in"], W["g_cq"], W["w_uq"], W["g_ckv"], W["w_uk"], W["w_uv"],
      cos, sina, sinb, W["g_qn_mla"], W["g_kn_mla"], W["g_qn_mem"])


def _mem_kv_kernel(mem_ref, gmem_ref, w_ref, gkn_ref, km_ref, vm_ref):
    hn = (_rms(mem_ref[0], D_MODEL) * gmem_ref[...]).astype(BF16)
    kv = _dot(hn, w_ref[...])
    for h in range(MEM_HEADS):
        kh = kv[:, h * LANES:(h + 1) * LANES]
        km_ref[0, h] = (_rms(kh, MEM_HEAD_DIM) * gkn_ref[...]).astype(BF16)
        vm_ref[0, h] = kv[:, MEM_WIDTH + h * LANES:MEM_WIDTH + (h + 1) * LANES].astype(BF16)


def _mem_kv(mem, l, W):
    B, M, _ = mem.shape
    out = pl.BlockSpec((1, MEM_HEADS, M, LANES), lambda b: (b, 0, 0, 0))
    shape = jax.ShapeDtypeStruct((B, MEM_HEADS, M, LANES), BF16)
    return pl.pallas_call(
        _mem_kv_kernel,
        grid=(B,),
        in_specs=[pl.BlockSpec((1, M, D_MODEL), lambda b: (b, 0, 0)), _layer_spec(l, (1, D_MODEL)),
                  _layer_spec(l, (D_MODEL, 2 * MEM_WIDTH)), _layer_spec(l, (1, LANES))],
        out_specs=[out, out],
        out_shape=[shape, shape],
        compiler_params=_params("parallel"),
        name="mem_kv",
    )(mem, W["g_mem"], W["w_mem_kv"], W["g_kn_mem"])


def _pool_bands():
    i = np.arange(POOL_ROWS)[:, None]
    r = np.arange(-POOL_HALO, POOL_ROWS + POOL_HALO)[None, :]
    bands = [((r - i >= -(w // 2)) & (r - i < w // 2)).astype(np.float32) for w in POOL_WINDOWS]
    return jnp.asarray(np.stack(bands), BF16)


def _pool_mix(seq_len, row0, has_prev, has_next, cur_ref, prev_ref, next_ref, band_ref, pw_ref, ps_ref):
    tm = cur_ref.shape[1]
    prev = jnp.where(has_prev, prev_ref[0], jnp.zeros_like(prev_ref[0]))
    nxt = jnp.where(has_next, next_ref[0], jnp.zeros_like(next_ref[0]))
    ext = jnp.concatenate([prev, cur_ref[0], nxt], axis=0)
    out = []
    for r in range(0, tm, POOL_ROWS):
        win = ext[r:r + POOL_ROWS + 2 * POOL_HALO]
        t = row0 + r + lax.broadcasted_iota(jnp.int32, (POOL_ROWS, 1), 0)
        groups = []
        for g, w in enumerate(POOL_WINDOWS):
            sl = slice(g * POOL_GROUP_DIM, (g + 1) * POOL_GROUP_DIM)
            count = jnp.minimum(t + w // 2, seq_len) - jnp.maximum(t - w // 2, 0)
            u = win[POOL_HALO:POOL_HALO + POOL_ROWS, sl].astype(F32)
            d = _dot(band_ref[g], win[:, sl]) / count.astype(F32) - u
            groups.append(_dot(d.astype(BF16), pw_ref[g]) * ps_ref[:, sl])
        out.append(jnp.concatenate(groups, axis=-1).astype(BF16))
    return jnp.concatenate(out, axis=0)


def _softmax_pv(s, v):
    m = jnp.max(s, axis=-1, keepdims=True)
    p = jnp.exp(s - m)
    l = jnp.sum(p, axis=-1, keepdims=True)
    return _dot(p.astype(BF16), v) / l


def _mla_attn_kernel(seq_len, tk, q_ref, k_ref, v_ref, o_ref):
    outs = []
    for hh in range(2):
        q = q_ref[0, hh]
        m = acc = None
        for c in range(seq_len // tk):
            ks = slice(c * tk, (c + 1) * tk)
            s = lax.dot_general(q, k_ref[0, hh, ks, :], NT_DIMS, preferred_element_type=F32)
            mc = jnp.max(s, axis=-1, keepdims=True)
            m_new = mc if m is None else jnp.maximum(m, mc)
            pv = _dot(jnp.exp2(s - m_new).astype(BF16), v_ref[0, hh, ks, :])
            acc = pv if acc is None else acc * jnp.exp2(m - m_new) + pv
            m = m_new
        outs.append(acc * (1.0 / acc[:, V_DIM:V_DIM + 1]))
    lane = lax.broadcasted_iota(jnp.int32, outs[0].shape, 1)
    o_ref[0] = jnp.where(lane < V_DIM, outs[0], pltpu.roll(outs[1], V_DIM, 1)).astype(BF16)


def _mla_attn(q, k, v, tq, tk):
    B, H, S, _ = q.shape
    kv = pl.BlockSpec((1, 2, S, LANES), lambda b, p, i: (b, p, 0, 0))
    return pl.pallas_call(
        functools.partial(_mla_attn_kernel, S, tk),
        grid=(B, H // 2, S // tq),
        in_specs=[pl.BlockSpec((1, 2, tq, LANES), lambda b, p, i: (b, p, i, 0)), kv, kv],
        out_specs=pl.BlockSpec((1, tq, 2 * V_DIM), lambda b, p, i: (b, i, p)),
        out_shape=jax.ShapeDtypeStruct((B, S, H * V_DIM), BF16),
        compiler_params=_params("parallel", "parallel", "arbitrary"),
        name="mla_attn",
    )(q, k, v)


def _merge_kernel(is_moe, seq_len, up_ref, prev_ref, next_ref, band_ref, pw_ref, ps_ref,
                  ym_ref, qm_ref, km_ref, vm_ref, gate_ref, wbr_ref, wout_ref,
                  x_ref, gffn_ref, xo_ref, *hf_refs):
    i = pl.program_id(1)
    y_pool = _pool_mix(seq_len, i * up_ref.shape[1], i > 0, i < pl.num_programs(1) - 1,
                       up_ref, prev_ref, next_ref, band_ref, pw_ref, ps_ref)

    mem = []
    for h in range(MEM_HEADS):
        s = lax.dot_general(qm_ref[0, h], km_ref[0, h], NT_DIMS, preferred_element_type=F32)
        mem.append(_softmax_pv(s, vm_ref[0, h]).astype(BF16))
    y_mem = jnp.concatenate(mem, axis=-1)

    branches = (y_pool, ym_ref[0], y_mem)
    merged = None
    for c, y in enumerate(branches):
        term = gate_ref[0, :, c * D_MODEL:(c + 1) * D_MODEL].astype(F32) * _dot(y, wbr_ref[c])
        merged = term if merged is None else merged + term
    x_new = x_ref[0] + _dot(merged.astype(BF16), wout_ref[...])
    xo_ref[0] = x_new
    hf = _rms(x_new, D_MODEL) * gffn_ref[...]
    if is_moe:
        for c, ref in enumerate(hf_refs):
            ref[0] = hf[:, c * SC_COLS:(c + 1) * SC_COLS]
    else:
        hf_refs[0][0] = hf.astype(BF16)


def _merge(is_moe, up, ym, qm, km, vm, gate, x, l, W, tm):
    B, S, _ = x.shape
    M = km.shape[2]
    row = lambda w: pl.BlockSpec((1, tm, w), lambda b, i: (b, i, 0))
    memkv = pl.BlockSpec((1, MEM_HEADS, M, LANES), lambda b, i: (b, 0, 0, 0))
    per = tm // POOL_HALO
    last = S // POOL_HALO - 1
    halo = lambda index: pl.BlockSpec((1, POOL_HALO, POOL_WIDTH), lambda b, i: (b, index(i), 0))
    in_specs = [row(POOL_WIDTH), halo(lambda i: jnp.maximum(i * per - 1, 0)),
                halo(lambda i: jnp.minimum((i + 1) * per, last)),
                _const_spec((N_POOL_GROUPS, POOL_ROWS, POOL_ROWS + 2 * POOL_HALO)),
                _layer_spec(l, (N_POOL_GROUPS, POOL_GROUP_DIM, POOL_GROUP_DIM)), _layer_spec(l, (1, POOL_WIDTH)),
                row(MLA_HEADS * V_DIM),
                pl.BlockSpec((1, MEM_HEADS, tm, LANES), lambda b, i: (b, 0, i, 0)), memkv, memkv,
                row(N_BRANCH * D_MODEL), _layer_spec(l, (N_BRANCH, POOL_WIDTH, D_MODEL)),
                _layer_spec(l, (D_MODEL, D_MODEL)), row(D_MODEL), _layer_spec(l, (1, D_MODEL))]
    out_specs = [row(D_MODEL)]
    out_shape = [jax.ShapeDtypeStruct((B, S, D_MODEL), F32)]
    if is_moe:
        out_specs += [row(SC_COLS)] * SC_CHUNKS
        out_shape += [jax.ShapeDtypeStruct((B, S, SC_COLS), F32)] * SC_CHUNKS
    else:
        out_specs.append(row(D_MODEL))
        out_shape.append(jax.ShapeDtypeStruct((B, S, D_MODEL), BF16))
    return pl.pallas_call(
        functools.partial(_merge_kernel, is_moe, S),
        grid=(B, S // tm),
        in_specs=in_specs, out_specs=out_specs, out_shape=out_shape,
        compiler_params=_params("parallel", "parallel"),
        name="merge_moe" if is_moe else "merge",
    )(up, up, up, _pool_bands(), W["pool_w"], W["pool_scale"],
      ym, qm, km, vm, gate, W["w_br"], W["w_out"], x, W["g_ffn"])


def _swiglu_chunk(h, wg, wu):
    g = _dot(h, wg)
    return g * jax.nn.sigmoid(g) * _dot(h, wu)


def _swiglu_down(h, wgu_ref, wd_ref, acc):
    for j in range(D_FF // FF_CHUNK):
        lo, hi = j * FF_CHUNK, (j + 1) * FF_CHUNK
        a = _swiglu_chunk(h, wgu_ref[:, lo:hi], wgu_ref[:, D_FF + lo:D_FF + hi])
        acc = acc + _dot(a.astype(BF16), wd_ref[lo:hi, :])
    return acc


def _ffn_kernel(hf_ref, wgu_ref, wd_ref, x_ref, o_ref):
    o_ref[...] = _swiglu_down(hf_ref[...], wgu_ref, wd_ref, x_ref[...])


def _ffn(hf, x, l, W, tm):
    N = hf.shape[0]
    row = pl.BlockSpec((tm, D_MODEL), lambda i: (i, 0))
    return pl.pallas_call(
        _ffn_kernel,
        grid=(N // tm,),
        in_specs=[row, _layer_spec(l // 2, (D_MODEL, 2 * D_FF)), _layer_spec(l // 2, (D_FF, D_MODEL)), row],
        out_specs=row,
        out_shape=jax.ShapeDtypeStruct((N, D_MODEL), F32),
        compiler_params=_params("parallel"),
        name="ffn",
    )(hf, W["w_gu_dense"], W["w_down_dense"], x)


def _router_kernel(*refs):
    hf_refs, (wrh_ref, wrl_ref, ridx_ref, rw_ref) = refs[:SC_CHUNKS], refs[SC_CHUNKS:]
    logits = None
    for c, hf_ref in enumerate(hf_refs):
        rows = slice(c * SC_COLS, (c + 1) * SC_COLS)
        hf = hf_ref[...]
        hi = hf.astype(BF16)
        lo = (hf - hi.astype(F32)).astype(BF16)
        part = _dot(hi, wrh_ref[rows, :]) + _dot(lo, wrh_ref[rows, :]) + _dot(hi, wrl_ref[rows, :])
        logits = part if logits is None else logits + part
    lane = lax.broadcasted_iota(jnp.int32, logits.shape, 1)
    lg = jnp.where(lane < N_EXPERTS, logits, -jnp.inf)
    m1 = jnp.max(lg, axis=-1, keepdims=True)
    i1 = jnp.min(jnp.where(lg == m1, lane, LANES), axis=-1, keepdims=True)
    lg2 = jnp.where(lane == i1, -jnp.inf, lg)
    m2 = jnp.max(lg2, axis=-1, keepdims=True)
    i2 = jnp.min(jnp.where(lg2 == m2, lane, LANES), axis=-1, keepdims=True)
    e = jnp.exp(m2 - m1)
    w1 = 1.0 / (1.0 + e)
    ridx_ref[...] = jnp.where(lane == 0, i1, jnp.where(lane == 1, i2, 0))
    rw_ref[...] = jnp.where(lane == 0, w1, jnp.where(lane == 1, e * w1, 0.0))


def _router(hf_chunks, l, W, tr):
    n = hf_chunks[0].shape[0]
    out = pl.BlockSpec((tr, LANES), lambda i: (i, 0))
    return pl.pallas_call(
        _router_kernel,
        grid=(n // tr,),
        in_specs=[pl.BlockSpec((tr, SC_COLS), lambda i: (i, 0))] * SC_CHUNKS
                 + [_layer_spec(l // 2, (D_MODEL, LANES)), _layer_spec(l // 2, (D_MODEL, LANES))],
        out_specs=[out, out],
        out_shape=[jax.ShapeDtypeStruct((n, LANES), jnp.int32), jax.ShapeDtypeStruct((n, LANES), F32)],
        compiler_params=_params("parallel"),
        name="router",
    )(*hf_chunks, W["w_router_hi"], W["w_router_lo"])


def _route_plan(ridx, tmx):
    n = ridx.shape[0]
    e = ridx[:, :2].T.reshape(2 * n)
    onehot = (e[:, None] == jnp.arange(N_EXPERTS, dtype=jnp.int32)[None, :]).astype(jnp.int32)
    csum = jnp.cumsum(onehot, axis=0)
    rank = jnp.sum((csum - onehot) * onehot, axis=-1)
    counts = csum[-1]
    padded = ((counts + tmx - 1) // tmx) * tmx
    ends = jnp.cumsum(padded)
    starts = ends - padded
    dest = jnp.sum(onehot * starts[None, :], axis=-1) + rank
    n_tiles = (2 * n) // tmx + N_EXPERTS
    tile_row0 = jnp.arange(n_tiles, dtype=jnp.int32) * tmx
    tile_e = jnp.minimum(jnp.sum((tile_row0[:, None] >= ends[None, :]).astype(jnp.int32), axis=-1),
                         N_EXPERTS - 1)
    group_end = jnp.take(starts + counts, tile_e)
    tile_valid = jnp.where(tile_row0 < ends[-1], jnp.clip(group_end - tile_row0, 0, tmx), 0)
    return dest.reshape(2, n).astype(jnp.int32), tile_e.astype(jnp.int32), tile_valid.astype(jnp.int32), n_tiles * tmx


def _sc_mesh():
    return plsc.VectorSubcoreMesh(core_axis_name="core", subcore_axis_name="subcore")


def _sc_scatter_rows(chunks, dest, n_out):
    n, width = chunks[0].shape
    per_choice = n // SC_WINDOW
    out_type = [jax.ShapeDtypeStruct((n_out, width), chunks[0].dtype)] * len(chunks)

    @pl.kernel(out_type=out_type, mesh=_sc_mesh(), scratch_types=[])
    def scatter(*refs):
        x_refs, i_hbm, o_refs = refs[:len(chunks)], refs[len(chunks)], refs[len(chunks) + 1:]
        for x_hbm, o_hbm in zip(x_refs, o_refs):
            def body(x_vmem, i_vmem, o_hbm=o_hbm):
                pltpu.sync_copy(x_vmem, o_hbm.at[i_vmem.at[0]])

            pltpu.emit_pipeline(
                body,
                grid=(2 * per_choice,),
                in_specs=[pl.BlockSpec((SC_WINDOW, width), lambda i: (i % per_choice, 0)),
                          pl.BlockSpec((1, SC_WINDOW), lambda i: (0, i))],
                out_specs=[],
                core_axis_name=("core", "subcore"),
                dimension_semantics=(pltpu.PARALLEL,),
            )(x_hbm, i_hbm)

    return scatter(*chunks, dest.reshape(1, 2 * n))


def _sc_gather_rows(chunks, idx):
    n = idx.shape[0]
    width = chunks[0].shape[1]
    out_type = [jax.ShapeDtypeStruct((n, width), chunks[0].dtype)] * len(chunks)

    @pl.kernel(out_type=out_type, mesh=_sc_mesh(), scratch_types=[])
    def gather(*refs):
        x_refs, i_hbm, o_refs = refs[:len(chunks)], refs[len(chunks)], refs[len(chunks) + 1:]
        for x_hbm, o_hbm in zip(x_refs, o_refs):
            def body(i_vmem, o_vmem, x_hbm=x_hbm):
                pltpu.sync_copy(x_hbm.at[i_vmem.at[0]], o_vmem)

            pltpu.emit_pipeline(
                body,
                grid=(n // SC_WINDOW,),
                in_specs=[pl.BlockSpec((1, SC_WINDOW), lambda i: (0, i))],
                out_specs=[pl.BlockSpec((SC_WINDOW, width), lambda i: (i, 0))],
                core_axis_name=("core", "subcore"),
                dimension_semantics=(pltpu.PARALLEL,),
            )(i_hbm, o_hbm)

    return gather(*chunks, idx.reshape(1, n))


def _pack_bf16_pair(lo, hi):
    bits = lambda a: lax.bitcast_convert_type(a.astype(BF16).astype(F32), jnp.uint32)
    return (bits(lo) >> 16) | (bits(hi) & jnp.uint32(0xFFFF0000))


def _unpack_bf16_pair(packed):
    f32 = lambda a: lax.bitcast_convert_type(a, F32)
    return f32(packed << 16), f32(packed & jnp.uint32(0xFFFF0000))


def _expert_sum(x, rw, y_refs):
    w0, w1 = rw[:, 0:1], rw[:, 1:2]
    cols = []
    for c in range(SC_CHUNKS // 2):
        lo0, hi0 = _unpack_bf16_pair(y_refs[2 * c][...])
        lo1, hi1 = _unpack_bf16_pair(y_refs[2 * c + 1][...])
        cols += [w0 * lo0 + w1 * lo1, w0 * hi0 + w1 * hi1]
    return x + jnp.concatenate(cols, axis=-1)


def _moe_group_kernel(te_ref, tv_ref, *refs):
    xs_refs, (wgu_ref, wd_ref) = refs[:SC_CHUNKS], refs[SC_CHUNKS:SC_CHUNKS + 2]
    ys_refs = refs[SC_CHUNKS + 2:]
    valid = tv_ref[pl.program_id(0)]

    @pl.when(valid > 0)
    def _():
        row = lax.broadcasted_iota(jnp.int32, xs_refs[0].shape, 0)
        h = jnp.concatenate([jnp.where(row < valid, xs_ref[...], 0.0).astype(BF16) for xs_ref in xs_refs], axis=-1)
        y = _swiglu_down(h, wgu_ref.at[0], wd_ref.at[0], jnp.zeros((h.shape[0], D_MODEL), F32))
        for c, ys_ref in enumerate(ys_refs):
            lo = y[:, 2 * c * SC_COLS:(2 * c + 1) * SC_COLS]
            hi = y[:, (2 * c + 1) * SC_COLS:(2 * c + 2) * SC_COLS]
            ys_ref[...] = _pack_bf16_pair(lo, hi)


def _moe_group(xs, tile_e, tile_valid, l, W, tmx):
    n_rows = xs[0].shape[0]
    chunk = pl.BlockSpec((tmx, SC_COLS), lambda t, te, tv: (t, 0))
    grid_spec = pltpu.PrefetchScalarGridSpec(
        num_scalar_prefetch=2,
        grid=(n_rows // tmx,),
        in_specs=[chunk] * SC_CHUNKS + [
            pl.BlockSpec((None, 1, D_MODEL, 2 * D_FF), lambda t, te, tv: (l // 2, te[t], 0, 0)),
            pl.BlockSpec((None, 1, D_FF, D_MODEL), lambda t, te, tv: (l // 2, te[t], 0, 0))],
        out_specs=[chunk] * (SC_CHUNKS // 2),
    )
    return pl.pallas_call(
        _moe_group_kernel,
        grid_spec=grid_spec,
        out_shape=[jax.ShapeDtypeStruct((n_rows, SC_COLS), jnp.uint32)] * (SC_CHUNKS // 2),
        compiler_params=_params("arbitrary"),
        name="moe_group",
    )(tile_e, tile_valid, *xs, W["w_gu_moe"], W["w_down_moe"])


def _moe_combine_kernel(x_ref, rw_ref, *refs):
    y_refs, o_ref = refs[:-1], refs[-1]
    o_ref[0] = _expert_sum(x_ref[0], rw_ref[0], [y.at[0, 0] for y in y_refs])


def _expert_args(yg, rw, batch, seq_len, tm):
    args = [rw.reshape(batch, seq_len, LANES)]
    specs = [pl.BlockSpec((1, tm, LANES), lambda b, i: (b, i, 0))]
    for y in yg:
        y4 = y.reshape(2, batch, seq_len, SC_COLS)
        for choice in range(2):
            args.append(y4)
            specs.append(pl.BlockSpec((1, 1, tm, SC_COLS), lambda b, i, choice=choice: (choice, b, i, 0)))
    return args, specs


def _moe_combine(x, yg, rw, tm):
    B, S, _ = x.shape
    row = pl.BlockSpec((1, tm, D_MODEL), lambda b, i: (b, i, 0))
    args, specs = _expert_args(yg, rw, B, S, tm)
    return pl.pallas_call(
        _moe_combine_kernel,
        grid=(B, S // tm),
        in_specs=[row] + specs,
        out_specs=row,
        out_shape=jax.ShapeDtypeStruct((B, S, D_MODEL), F32),
        compiler_params=_params("parallel", "parallel"),
        name="moe_combine",
    )(x, *args)


def _moe(hf_chunks, l, W):
    n = hf_chunks[0].shape[0]
    ridx, rw = _router(hf_chunks, l, W, min(2048, n))
    dest, tile_e, tile_valid, n_rows = _route_plan(ridx, MOE_TILE)
    xs = _sc_scatter_rows(hf_chunks, dest, n_rows)
    ys = _moe_group(xs, tile_e, tile_valid, l, W, MOE_TILE)
    yg = _sc_gather_rows(ys, dest.reshape(2 * n))
    return yg, rw


def _rope_tables(seq_len):
    half = QK_ROPE // 2
    freqs = jnp.power(ROPE_THETA, -jnp.arange(half, dtype=F32) / half)
    ang = jnp.arange(seq_len).astype(F32)[:, None] * freqs[None, :]
    c, s = jnp.cos(ang), jnp.sin(ang)
    z = lambda n: jnp.zeros((seq_len, n), F32)
    o = lambda n: jnp.ones((seq_len, n), F32)
    tail = LANES - QK_NOPE - QK_ROPE
    cos = jnp.concatenate([o(QK_NOPE), c, c, o(tail)], axis=-1)
    sina = jnp.concatenate([z(QK_NOPE), -s, z(half), z(tail)], axis=-1)
    sinb = jnp.concatenate([z(QK_NOPE), z(half), s, z(tail)], axis=-1)
    return cos, sina, sinb


def _pad_last(a, width):
    return jnp.pad(a, [(0, 0)] * (a.ndim - 1) + [(0, width - a.shape[-1])])


def _cast_kernel(x_ref, o_ref):
    o_ref[...] = x_ref[...].astype(o_ref.dtype)


def _to_bf16(a):
    flat = a.reshape(-1, a.shape[-1])
    rows, cols = flat.shape
    bf16_rows = 16
    cap = CAST_BLOCK_BYTES // (4 * cols)
    tr = max(t for t in range(bf16_rows, cap + 1, bf16_rows) if rows % t == 0)
    block = pl.BlockSpec((tr, cols), lambda i: (i, 0))
    out = pl.pallas_call(
        _cast_kernel,
        grid=(rows // tr,),
        in_specs=[block],
        out_specs=block,
        out_shape=jax.ShapeDtypeStruct(flat.shape, BF16),
        compiler_params=_params("parallel"),
        name="cast_bf16",
    )(flat)
    return out.reshape(a.shape)


def _prep_weights(g_mix, w_in, pool_w, pool_scale, g_cq, w_uq, g_ckv, w_ukv, g_qn_mla, g_kn_mla, g_mem,
                  w_mem_kv, g_qn_mem, g_kn_mem, w_br, w_out, g_ffn, w_gu_dense, w_down_dense, w_router,
                  w_gu_moe, w_down_moe):
    depth = w_in.shape[0]
    kr_lo = POOL_WIDTH + Q_LORA + KV_LORA
    kr_hi = kr_lo + QK_ROPE
    wi = w_in.astype(BF16)
    zc = lambda n: jnp.zeros((depth, D_MODEL, n), BF16)
    w_in_p = jnp.concatenate([wi[..., :kr_lo], zc(QK_NOPE), wi[..., kr_lo:kr_hi], zc(LANES - QK_NOPE - QK_ROPE),
                              wi[..., kr_hi:]], axis=-1)
    w_uq_p = _pad_last(w_uq.astype(BF16).reshape(depth, Q_LORA, MLA_HEADS, QK_DIM), LANES)
    ukv = w_ukv.astype(BF16).reshape(depth, KV_LORA, MLA_HEADS, QK_NOPE + V_DIM)
    w_uk_p = _pad_last(ukv[..., :QK_NOPE], LANES)
    w_uv_p = _pad_last(ukv[..., QK_NOPE:], LANES)
    row = lambda a: a[:, None, :]
    wr = _pad_last(w_router, LANES)
    wr_hi = wr.astype(BF16)
    return {
        "g_mix": row(g_mix), "w_in": w_in_p,
        "g_cq": row(g_cq), "w_uq": w_uq_p.reshape(depth, Q_LORA, MLA_HEADS * LANES),
        "g_ckv": row(g_ckv), "w_uk": w_uk_p.reshape(depth, KV_LORA, MLA_HEADS * LANES),
        "w_uv": w_uv_p.reshape(depth, KV_LORA, MLA_HEADS * LANES),
        "g_qn_mla": _pad_last(row(g_qn_mla), LANES), "g_kn_mla": _pad_last(row(g_kn_mla), LANES),
        "g_qn_mem": row(g_qn_mem), "g_kn_mem": row(g_kn_mem),
        "g_mem": row(g_mem), "w_mem_kv": w_mem_kv.astype(BF16),
        "pool_w": pool_w.astype(BF16), "pool_scale": row(pool_scale),
        "w_br": w_br.astype(BF16), "w_out": w_out.astype(BF16), "g_ffn": row(g_ffn),
        "w_gu_dense": _to_bf16(w_gu_dense), "w_down_dense": _to_bf16(w_down_dense),
        "w_router_hi": wr_hi, "w_router_lo": (wr - wr_hi.astype(F32)).astype(BF16),
        "w_gu_moe": _to_bf16(w_gu_moe), "w_down_moe": _to_bf16(w_down_moe),
    }


def _trunk(x, mem, W):
    B, S, _ = x.shape
    tm = min(512, S)
    tq = min(1024, S)
    tk = min(max(S // 2, 1024), S)
    tn = min(512, B * S)
    tables = _rope_tables(S)
    pending = None
    for l in range(DEPTH):
        if pending:
            up, q, k, v, qm, gate, x = _mix_in(x, pending, l, W, tables, tm)
        else:
            up, q, k, v, qm, gate = _mix_in(x, None, l, W, tables, tm)
        km, vm = _mem_kv(mem, l, W)
        ym = _mla_attn(q, k, v, tq, tk)
        if l % 2 == 0:
            x_mid, hf = _merge(False, up, ym, qm, km, vm, gate, x, l, W, tm)
            x = _ffn(hf.reshape(B * S, D_MODEL), x_mid.reshape(B * S, D_MODEL), l, W, tn).reshape(B, S, D_MODEL)
            pending = None
        else:
            x, *hf = _merge(True, up, ym, qm, km, vm, gate, x, l, W, tm)
            pending = _moe([h.reshape(B * S, SC_COLS) for h in hf], l, W)
    if pending:
        x = _moe_combine(x, *pending, tm)
    return x


def kernel(x_prompt, x_sample, mem_prompt, mem_sample, g_mix, w_in, pool_w, pool_scale, g_cq, w_uq, g_ckv, w_ukv, g_qn_mla, g_kn_mla, g_mem, w_mem_kv, g_qn_mem, g_kn_mem, w_br, w_out, g_ffn, w_gu_dense, w_down_dense, w_router, w_gu_moe, w_down_moe):
    W = _prep_weights(g_mix, w_in, pool_w, pool_scale, g_cq, w_uq, g_ckv, w_ukv, g_qn_mla, g_kn_mla, g_mem,
                      w_mem_kv, g_qn_mem, g_kn_mem, w_br, w_out, g_ffn, w_gu_dense, w_down_dense, w_router,
                      w_gu_moe, w_down_moe)
    y_prompt = _trunk(x_prompt, mem_prompt, W)
    y_sample = _trunk(x_sample, mem_sample, W)
    return (y_prompt, y_sample)
```

```python
import functools

import numpy as np

import jax
import jax.numpy as jnp
from jax import lax
from jax.experimental import pallas as pl
from jax.experimental.pallas import tpu as pltpu
from jax.experimental.pallas import tpu_sc as plsc

D_MODEL = 1024
DEPTH = 4
POOL_WINDOWS = (2, 4, 8, 16)
N_POOL_GROUPS = 4
POOL_GROUP_DIM = 128
POOL_WIDTH = 512
MLA_HEADS = 8
QK_NOPE = 64
QK_ROPE = 32
QK_DIM = 96
V_DIM = 64
Q_LORA = 384
KV_LORA = 256
ROPE_THETA = 10000.0
MEM_TOKENS = 256
MEM_HEADS = 4
MEM_HEAD_DIM = 128
MEM_WIDTH = 512
N_BRANCH = 3
D_FF = 2816
N_EXPERTS = 8
EPS = 1e-6
LOG2_E = 1.4426950408889634

LANES = 128
POOL_HALO = 16
POOL_ROWS = 256
VMEM_LIMIT_BYTES = 56 * 1024 * 1024
MOE_TILE = 512
FF_CHUNK = 256
SC_WINDOW = 128
SC_COLS = 256
SC_CHUNKS = D_MODEL // SC_COLS
CAST_BLOCK_BYTES = 12 * 1024 * 1024
CAST_STREAMS = 4

C_POOL = 0
C_CQ = C_POOL + POOL_WIDTH
C_CKV = C_CQ + Q_LORA
C_KR = C_CKV + KV_LORA
C_QM = C_KR + LANES
C_GATE = C_QM + MEM_WIDTH
IN_PACKED = C_GATE + N_BRANCH * D_MODEL

BF16 = jnp.bfloat16
F32 = jnp.float32
NT_DIMS = (((1,), (1,)), ((), ()))


def _const_spec(shape):
    zeros = (0,) * len(shape)
    return pl.BlockSpec(shape, lambda *_: zeros, pipeline_mode=pl.Buffered(1))


def _layer_spec(l, shape):
    zeros = (0,) * len(shape)
    return pl.BlockSpec((None,) + tuple(shape), lambda *_: (l,) + zeros, pipeline_mode=pl.Buffered(1))


def _params(*sem):
    return pltpu.CompilerParams(dimension_semantics=sem, vmem_limit_bytes=VMEM_LIMIT_BYTES)


def _rms(x, width):
    return x * lax.rsqrt(jnp.sum(x * x, axis=-1, keepdims=True) * (1.0 / width) + EPS)


def _dot(a, b):
    return jnp.dot(a, b, preferred_element_type=F32)


def _mix_in_kernel(n_pending, x_ref, *refs):
    pending, refs = refs[:n_pending], refs[n_pending:]
    (gmix_ref, win_ref, gcq_ref, wuq_ref, gckv_ref, wuk_ref, wuv_ref, cos_ref, sina_ref, sinb_ref,
     gqn_ref, gkn_ref, gqm_ref, up_ref, q_ref, k_ref, v_ref, qm_ref, gate_ref) = refs[:19]
    x = x_ref[0]
    if n_pending:
        x = _expert_sum(x, pending[0][0], [y.at[0, 0] for y in pending[1:]])
        refs[19][0] = x
    hn = (_rms(x, D_MODEL) * gmix_ref[...]).astype(BF16)

    def proj(lo, hi):
        return _dot(hn, win_ref[:, lo:hi])

    cos, sina, sinb = cos_ref[...], sina_ref[...], sinb_ref[...]

    def rope(t):
        return t * cos + pltpu.roll(t, LANES - QK_ROPE // 2, 1) * sina + pltpu.roll(t, QK_ROPE // 2, 1) * sinb

    up_ref[0] = proj(C_POOL, C_CQ).astype(BF16)

    cqn = (_rms(proj(C_CQ, C_CKV), Q_LORA) * gcq_ref[...]).astype(BF16)
    qf = _dot(cqn, wuq_ref[...])
    gq = gqn_ref[...] * (QK_DIM ** -0.5 * LOG2_E)
    for h in range(MLA_HEADS):
        qh = rope(qf[:, h * LANES:(h + 1) * LANES])
        q_ref[0, h] = (_rms(qh, QK_DIM) * gq).astype(BF16)

    ckvn = (_rms(proj(C_CKV, C_KR), KV_LORA) * gckv_ref[...]).astype(BF16)
    kf = _dot(ckvn, wuk_ref[...])
    kr = rope(proj(C_KR, C_QM))
    gk = gkn_ref[...]
    for h in range(MLA_HEADS):
        kh = kf[:, h * LANES:(h + 1) * LANES] + kr
        k_ref[0, h] = (_rms(kh, QK_DIM) * gk).astype(BF16)
    vf = _dot(ckvn, wuv_ref[...])
    lane = lax.broadcasted_iota(jnp.int32, (vf.shape[0], LANES), 1)
    for h in range(MLA_HEADS):
        v_ref[0, h] = jnp.where(lane < V_DIM, vf[:, h * LANES:(h + 1) * LANES], 1.0).astype(BF16)

    qm = proj(C_QM, C_GATE)
    gm = gqm_ref[...] * (MEM_HEAD_DIM ** -0.5)
    for h in range(MEM_HEADS):
        qh = qm[:, h * LANES:(h + 1) * LANES]
        qm_ref[0, h] = (_rms(qh, MEM_HEAD_DIM) * gm).astype(BF16)

    for c in range(N_BRANCH):
        gl = proj(C_GATE + c * D_MODEL, C_GATE + (c + 1) * D_MODEL)
        gate_ref[0, :, c * D_MODEL:(c + 1) * D_MODEL] = (0.5 * jnp.tanh(0.5 * gl) + 0.5).astype(BF16)


def _mix_in(x, pending, l, W, tables, tm):
    B, S, _ = x.shape
    cos, sina, sinb = tables
    row = lambda w: pl.BlockSpec((1, tm, w), lambda b, i: (b, i, 0))
    heads = lambda n: pl.BlockSpec((1, n, tm, LANES), lambda b, i: (b, 0, i, 0))
    tab = pl.BlockSpec((tm, LANES), lambda b, i: (i, 0))
    ls = functools.partial(_layer_spec, l)
    head_shape = jax.ShapeDtypeStruct((B, MLA_HEADS, S, LANES), BF16)
    pend_args, pend_specs = _expert_args(*pending, B, S, tm) if pending else ([], [])
    out_specs = [row(POOL_WIDTH), heads(MLA_HEADS), heads(MLA_HEADS), heads(MLA_HEADS),
                 heads(MEM_HEADS), row(N_BRANCH * D_MODEL)]
    out_shape = [jax.ShapeDtypeStruct((B, S, POOL_WIDTH), BF16), head_shape, head_shape, head_shape,
                 jax.ShapeDtypeStruct((B, MEM_HEADS, S, LANES), BF16),
                 jax.ShapeDtypeStruct((B, S, N_BRANCH * D_MODEL), BF16)]
    if pending:
        out_specs.append(row(D_MODEL))
        out_shape.append(jax.ShapeDtypeStruct((B, S, D_MODEL), F32))
    return pl.pallas_call(
        functools.partial(_mix_in_kernel, len(pend_args)),
        grid=(B, S // tm),
        in_specs=[row(D_MODEL)] + pend_specs + [
            ls((1, D_MODEL)), ls((D_MODEL, IN_PACKED)),
            ls((1, Q_LORA)), ls((Q_LORA, MLA_HEADS * LANES)),
            ls((1, KV_LORA)), ls((KV_LORA, MLA_HEADS * LANES)), ls((KV_LORA, MLA_HEADS * LANES)),
            tab, tab, tab,
            ls((1, LANES)), ls((1, LANES)), ls((1, LANES))],
        out_specs=out_specs,
        out_shape=out_shape,
        compiler_params=_params("parallel", "parallel"),
        name="mix_in",
    )(x, *pend_args, W["g_mix"], W["w_in"], W["g_cq"], W["w_uq"], W["g_ckv"], W["w_uk"], W["w_uv"],
      cos, sina, sinb, W["g_qn_mla"], W["g_kn_mla"], W["g_qn_mem"])


def _mem_kv_kernel(mem_ref, gmem_ref, w_ref, gkn_ref, km_ref, vm_ref):
    hn = (_rms(mem_ref[0], D_MODEL) * gmem_ref[...]).astype(BF16)
    kv = _dot(hn, w_ref[...])
    for h in range(MEM_HEADS):
        kh = kv[:, h * LANES:(h + 1) * LANES]
        km_ref[0, h] = (_rms(kh, MEM_HEAD_DIM) * gkn_ref[...]).astype(BF16)
        vm_ref[0, h] = kv[:, MEM_WIDTH + h * LANES:MEM_WIDTH + (h + 1) * LANES].astype(BF16)


def _mem_kv(mem, l, W):
    B, M, _ = mem.shape
    out = pl.BlockSpec((1, MEM_HEADS, M, LANES), lambda b: (b, 0, 0, 0))
    shape = jax.ShapeDtypeStruct((B, MEM_HEADS, M, LANES), BF16)
    return pl.pallas_call(
        _mem_kv_kernel,
        grid=(B,),
        in_specs=[pl.BlockSpec((1, M, D_MODEL), lambda b: (b, 0, 0)), _layer_spec(l, (1, D_MODEL)),
                  _layer_spec(l, (D_MODEL, 2 * MEM_WIDTH)), _layer_spec(l, (1, LANES))],
        out_specs=[out, out],
        out_shape=[shape, shape],
        compiler_params=_params("parallel"),
        name="mem_kv",
    )(mem, W["g_mem"], W["w_mem_kv"], W["g_kn_mem"])


def _pool_bands():
    i = np.arange(POOL_ROWS)[:, None]
    r = np.arange(-POOL_HALO, POOL_ROWS + POOL_HALO)[None, :]
    bands = [((r - i >= -(w // 2)) & (r - i < w // 2)).astype(np.float32) for w in POOL_WINDOWS]
    return jnp.asarray(np.stack(bands), BF16)


def _pool_mix(seq_len, row0, has_prev, has_next, cur_ref, prev_ref, next_ref, band_ref, pw_ref, ps_ref):
    tm = cur_ref.shape[1]
    prev = jnp.where(has_prev, prev_ref[0], jnp.zeros_like(prev_ref[0]))
    nxt = jnp.where(has_next, next_ref[0], jnp.zeros_like(next_ref[0]))
    ext = jnp.concatenate([prev, cur_ref[0], nxt], axis=0)
    out = []
    for r in range(0, tm, POOL_ROWS):
        win = ext[r:r + POOL_ROWS + 2 * POOL_HALO]
        t = row0 + r + lax.broadcasted_iota(jnp.int32, (POOL_ROWS, 1), 0)
        groups = []
        for g, w in enumerate(POOL_WINDOWS):
            sl = slice(g * POOL_GROUP_DIM, (g + 1) * POOL_GROUP_DIM)
            count = jnp.minimum(t + w // 2, seq_len) - jnp.maximum(t - w // 2, 0)
            u = win[POOL_HALO:POOL_HALO + POOL_ROWS, sl].astype(F32)
            d = _dot(band_ref[g], win[:, sl]) / count.astype(F32) - u
            groups.append(_dot(d.astype(BF16), pw_ref[g]) * ps_ref[:, sl])
        out.append(jnp.concatenate(groups, axis=-1).astype(BF16))
    return jnp.concatenate(out, axis=0)


def _softmax_pv(s, v):
    m = jnp.max(s, axis=-1, keepdims=True)
    p = jnp.exp(s - m)
    l = jnp.sum(p, axis=-1, keepdims=True)
    return _dot(p.astype(BF16), v) / l


def _mla_attn_kernel(seq_len, tk, q_ref, k_ref, v_ref, o_ref):
    outs = []
    for hh in range(2):
        q = q_ref[0, hh]
        m = acc = None
        for c in range(seq_len // tk):
            ks = slice(c * tk, (c + 1) * tk)
            s = lax.dot_general(q, k_ref[0, hh, ks, :], NT_DIMS, preferred_element_type=F32)
            mc = jnp.max(s, axis=-1, keepdims=True)
            m_new = mc if m is None else jnp.maximum(m, mc)
            pv = _dot(jnp.exp2(s - m_new).astype(BF16), v_ref[0, hh, ks, :])
            acc = pv if acc is None else acc * jnp.exp2(m - m_new) + pv
            m = m_new
        outs.append(acc * (1.0 / acc[:, V_DIM:V_DIM + 1]))
    lane = lax.broadcasted_iota(jnp.int32, outs[0].shape, 1)
    o_ref[0] = jnp.where(lane < V_DIM, outs[0], pltpu.roll(outs[1], V_DIM, 1)).astype(BF16)


def _mla_attn(q, k, v, tq, tk):
    B, H, S, _ = q.shape
    kv = pl.BlockSpec((1, 2, S, LANES), lambda b, p, i: (b, p, 0, 0))
    return pl.pallas_call(
        functools.partial(_mla_attn_kernel, S, tk),
        grid=(B, H // 2, S // tq),
        in_specs=[pl.BlockSpec((1, 2, tq, LANES), lambda b, p, i: (b, p, i, 0)), kv, kv],
        out_specs=pl.BlockSpec((1, tq, 2 * V_DIM), lambda b, p, i: (b, i, p)),
        out_shape=jax.ShapeDtypeStruct((B, S, H * V_DIM), BF16),
        compiler_params=_params("parallel", "parallel", "arbitrary"),
        name="mla_attn",
    )(q, k, v)


def _merge_kernel(is_moe, seq_len, up_ref, prev_ref, next_ref, band_ref, pw_ref, ps_ref,
                  ym_ref, qm_ref, km_ref, vm_ref, gate_ref, wbr_ref, wout_ref,
                  x_ref, gffn_ref, xo_ref, *hf_refs):
    i = pl.program_id(1)
    y_pool = _pool_mix(seq_len, i * up_ref.shape[1], i > 0, i < pl.num_programs(1) - 1,
                       up_ref, prev_ref, next_ref, band_ref, pw_ref, ps_ref)

    mem = []
    for h in range(MEM_HEADS):
        s = lax.dot_general(qm_ref[0, h], km_ref[0, h], NT_DIMS, preferred_element_type=F32)
        mem.append(_softmax_pv(s, vm_ref[0, h]).astype(BF16))
    y_mem = jnp.concatenate(mem, axis=-1)

    branches = (y_pool, ym_ref[0], y_mem)
    merged = None
    for c, y in enumerate(branches):
        term = gate_ref[0, :, c * D_MODEL:(c + 1) * D_MODEL].astype(F32) * _dot(y, wbr_ref[c])
        merged = term if merged is None else merged + term
    x_new = x_ref[0] + _dot(merged.astype(BF16), wout_ref[...])
    xo_ref[0] = x_new
    hf = _rms(x_new, D_MODEL) * gffn_ref[...]
    if is_moe:
        for c, ref in enumerate(hf_refs):
            ref[0] = hf[:, c * SC_COLS:(c + 1) * SC_COLS]
    else:
        hf_refs[0][0] = hf.astype(BF16)


def _merge(is_moe, up, ym, qm, km, vm, gate, x, l, W, tm):
    B, S, _ = x.shape
    M = km.shape[2]
    row = lambda w: pl.BlockSpec((1, tm, w), lambda b, i: (b, i, 0))
    memkv = pl.BlockSpec((1, MEM_HEADS, M, LANES), lambda b, i: (b, 0, 0, 0))
    per = tm // POOL_HALO
    last = S // POOL_HALO - 1
    halo = lambda index: pl.BlockSpec((1, POOL_HALO, POOL_WIDTH), lambda b, i: (b, index(i), 0))
    in_specs = [row(POOL_WIDTH), halo(lambda i: jnp.maximum(i * per - 1, 0)),
                halo(lambda i: jnp.minimum((i + 1) * per, last)),
                _const_spec((N_POOL_GROUPS, POOL_ROWS, POOL_ROWS + 2 * POOL_HALO)),
                _layer_spec(l, (N_POOL_GROUPS, POOL_GROUP_DIM, POOL_GROUP_DIM)), _layer_spec(l, (1, POOL_WIDTH)),
                row(MLA_HEADS * V_DIM),
                pl.BlockSpec((1, MEM_HEADS, tm, LANES), lambda b, i: (b, 0, i, 0)), memkv, memkv,
                row(N_BRANCH * D_MODEL), _layer_spec(l, (N_BRANCH, POOL_WIDTH, D_MODEL)),
                _layer_spec(l, (D_MODEL, D_MODEL)), row(D_MODEL), _layer_spec(l, (1, D_MODEL))]
    out_specs = [row(D_MODEL)]
    out_shape = [jax.ShapeDtypeStruct((B, S, D_MODEL), F32)]
    if is_moe:
        out_specs += [row(SC_COLS)] * SC_CHUNKS
        out_shape += [jax.ShapeDtypeStruct((B, S, SC_COLS), F32)] * SC_CHUNKS
    else:
        out_specs.append(row(D_MODEL))
        out_shape.append(jax.ShapeDtypeStruct((B, S, D_MODEL), BF16))
    return pl.pallas_call(
        functools.partial(_merge_kernel, is_moe, S),
        grid=(B, S // tm),
        in_specs=in_specs, out_specs=out_specs, out_shape=out_shape,
        compiler_params=_params("parallel", "parallel"),
        name="merge_moe" if is_moe else "merge",
    )(up, up, up, _pool_bands(), W["pool_w"], W["pool_scale"],
      ym, qm, km, vm, gate, W["w_br"], W["w_out"], x, W["g_ffn"])


def _swiglu_chunk(h, wg, wu):
    g = _dot(h, wg)
    return g * jax.nn.sigmoid(g) * _dot(h, wu)


def _swiglu_down(h, wgu_ref, wd_ref, acc):
    for j in range(D_FF // FF_CHUNK):
        lo, hi = j * FF_CHUNK, (j + 1) * FF_CHUNK
        a = _swiglu_chunk(h, wgu_ref[:, lo:hi], wgu_ref[:, D_FF + lo:D_FF + hi])
        acc = acc + _dot(a.astype(BF16), wd_ref[lo:hi, :])
    return acc


def _ffn_kernel(hf_ref, wgu_ref, wd_ref, x_ref, o_ref):
    o_ref[...] = _swiglu_down(hf_ref[...], wgu_ref, wd_ref, x_ref[...])


def _ffn(hf, x, l, W, tm):
    N = hf.shape[0]
    row = pl.BlockSpec((tm, D_MODEL), lambda i: (i, 0))
    return pl.pallas_call(
        _ffn_kernel,
        grid=(N // tm,),
        in_specs=[row, _layer_spec(l // 2, (D_MODEL, 2 * D_FF)), _layer_spec(l // 2, (D_FF, D_MODEL)), row],
        out_specs=row,
        out_shape=jax.ShapeDtypeStruct((N, D_MODEL), F32),
        compiler_params=_params("parallel"),
        name="ffn",
    )(hf, W["w_gu_dense"], W["w_down_dense"], x)


def _router_kernel(*refs):
    hf_refs, (wrh_ref, wrl_ref, ridx_ref, rw_ref) = refs[:SC_CHUNKS], refs[SC_CHUNKS:]
    logits = None
    for c, hf_ref in enumerate(hf_refs):
        rows = slice(c * SC_COLS, (c + 1) * SC_COLS)
        hf = hf_ref[...]
        hi = hf.astype(BF16)
        lo = (hf - hi.astype(F32)).astype(BF16)
        part = _dot(hi, wrh_ref[rows, :]) + _dot(lo, wrh_ref[rows, :]) + _dot(hi, wrl_ref[rows, :])
        logits = part if logits is None else logits + part
    lane = lax.broadcasted_iota(jnp.int32, logits.shape, 1)
    lg = jnp.where(lane < N_EXPERTS, logits, -jnp.inf)
    m1 = jnp.max(lg, axis=-1, keepdims=True)
    i1 = jnp.min(jnp.where(lg == m1, lane, LANES), axis=-1, keepdims=True)
    lg2 = jnp.where(lane == i1, -jnp.inf, lg)
    m2 = jnp.max(lg2, axis=-1, keepdims=True)
    i2 = jnp.min(jnp.where(lg2 == m2, lane, LANES), axis=-1, keepdims=True)
    e = jnp.exp(m2 - m1)
    w1 = 1.0 / (1.0 + e)
    ridx_ref[...] = jnp.where(lane == 0, i1, jnp.where(lane == 1, i2, 0))
    rw_ref[...] = jnp.where(lane == 0, w1, jnp.where(lane == 1, e * w1, 0.0))


def _router(hf_chunks, l, W, tr):
    n = hf_chunks[0].shape[0]
    out = pl.BlockSpec((tr, LANES), lambda i: (i, 0))
    return pl.pallas_call(
        _router_kernel,
        grid=(n // tr,),
        in_specs=[pl.BlockSpec((tr, SC_COLS), lambda i: (i, 0))] * SC_CHUNKS
                 + [_layer_spec(l // 2, (D_MODEL, LANES)), _layer_spec(l // 2, (D_MODEL, LANES))],
        out_specs=[out, out],
        out_shape=[jax.ShapeDtypeStruct((n, LANES), jnp.int32), jax.ShapeDtypeStruct((n, LANES), F32)],
        compiler_params=_params("parallel"),
        name="router",
    )(*hf_chunks, W["w_router_hi"], W["w_router_lo"])


def _route_plan(ridx, tmx):
    n = ridx.shape[0]
    e = ridx[:, :2].T.reshape(2 * n)
    onehot = (e[:, None] == jnp.arange(N_EXPERTS, dtype=jnp.int32)[None, :]).astype(jnp.int32)
    csum = jnp.cumsum(onehot, axis=0)
    rank = jnp.sum((csum - onehot) * onehot, axis=-1)
    counts = csum[-1]
    padded = ((counts + tmx - 1) // tmx) * tmx
    ends = jnp.cumsum(padded)
    starts = ends - padded
    dest = jnp.sum(onehot * starts[None, :], axis=-1) + rank
    n_tiles = (2 * n) // tmx + N_EXPERTS
    tile_row0 = jnp.arange(n_tiles, dtype=jnp.int32) * tmx
    tile_e = jnp.minimum(jnp.sum((tile_row0[:, None] >= ends[None, :]).astype(jnp.int32), axis=-1),
                         N_EXPERTS - 1)
    group_end = jnp.take(starts + counts, tile_e)
    tile_valid = jnp.where(tile_row0 < ends[-1], jnp.clip(group_end - tile_row0, 0, tmx), 0)
    return dest.reshape(2, n).astype(jnp.int32), tile_e.astype(jnp.int32), tile_valid.astype(jnp.int32), n_tiles * tmx


def _sc_mesh():
    return plsc.VectorSubcoreMesh(core_axis_name="core", subcore_axis_name="subcore")


def _sc_scatter_rows(chunks, dest, n_out):
    n, width = chunks[0].shape
    per_choice = n // SC_WINDOW
    out_type = [jax.ShapeDtypeStruct((n_out, width), chunks[0].dtype)] * len(chunks)

    @pl.kernel(out_type=out_type, mesh=_sc_mesh(), scratch_types=[])
    def scatter(*refs):
        x_refs, i_hbm, o_refs = refs[:len(chunks)], refs[len(chunks)], refs[len(chunks) + 1:]
        for x_hbm, o_hbm in zip(x_refs, o_refs):
            def body(x_vmem, i_vmem, o_hbm=o_hbm):
                pltpu.sync_copy(x_vmem, o_hbm.at[i_vmem.at[0]])

            pltpu.emit_pipeline(
                body,
                grid=(2 * per_choice,),
                in_specs=[pl.BlockSpec((SC_WINDOW, width), lambda i: (i % per_choice, 0)),
                          pl.BlockSpec((1, SC_WINDOW), lambda i: (0, i))],
                out_specs=[],
                core_axis_name=("core", "subcore"),
                dimension_semantics=(pltpu.PARALLEL,),
            )(x_hbm, i_hbm)

    return scatter(*chunks, dest.reshape(1, 2 * n))


def _sc_gather_rows(chunks, idx):
    n = idx.shape[0]
    width = chunks[0].shape[1]
    out_type = [jax.ShapeDtypeStruct((n, width), chunks[0].dtype)] * len(chunks)

    @pl.kernel(out_type=out_type, mesh=_sc_mesh(), scratch_types=[])
    def gather(*refs):
        x_refs, i_hbm, o_refs = refs[:len(chunks)], refs[len(chunks)], refs[len(chunks) + 1:]
        for x_hbm, o_hbm in zip(x_refs, o_refs):
            def body(i_vmem, o_vmem, x_hbm=x_hbm):
                pltpu.sync_copy(x_hbm.at[i_vmem.at[0]], o_vmem)

            pltpu.emit_pipeline(
                body,
                grid=(n // SC_WINDOW,),
                in_specs=[pl.BlockSpec((1, SC_WINDOW), lambda i: (0, i))],
                out_specs=[pl.BlockSpec((SC_WINDOW, width), lambda i: (i, 0))],
                core_axis_name=("core", "subcore"),
                dimension_semantics=(pltpu.PARALLEL,),
            )(i_hbm, o_hbm)

    return gather(*chunks, idx.reshape(1, n))


def _pack_bf16_pair(lo, hi):
    bits = lambda a: lax.bitcast_convert_type(a.astype(BF16).astype(F32), jnp.uint32)
    return (bits(lo) >> 16) | (bits(hi) & jnp.uint32(0xFFFF0000))


def _unpack_bf16_pair(packed):
    f32 = lambda a: lax.bitcast_convert_type(a, F32)
    return f32(packed << 16), f32(packed & jnp.uint32(0xFFFF0000))


def _expert_sum(x, rw, y_refs):
    w0, w1 = rw[:, 0:1], rw[:, 1:2]
    cols = []
    for c in range(SC_CHUNKS // 2):
        lo0, hi0 = _unpack_bf16_pair(y_refs[2 * c][...])
        lo1, hi1 = _unpack_bf16_pair(y_refs[2 * c + 1][...])
        cols += [w0 * lo0 + w1 * lo1, w0 * hi0 + w1 * hi1]
    return x + jnp.concatenate(cols, axis=-1)


def _moe_group_kernel(te_ref, tv_ref, *refs):
    xs_refs, (wgu_ref, wd_ref) = refs[:SC_CHUNKS], refs[SC_CHUNKS:SC_CHUNKS + 2]
    ys_refs = refs[SC_CHUNKS + 2:]
    valid = tv_ref[pl.program_id(0)]

    @pl.when(valid > 0)
    def _():
        row = lax.broadcasted_iota(jnp.int32, xs_refs[0].shape, 0)
        h = jnp.concatenate([jnp.where(row < valid, xs_ref[...], 0.0).astype(BF16) for xs_ref in xs_refs], axis=-1)
        y = _swiglu_down(h, wgu_ref.at[0], wd_ref.at[0], jnp.zeros((h.shape[0], D_MODEL), F32))
        for c, ys_ref in enumerate(ys_refs):
            lo = y[:, 2 * c * SC_COLS:(2 * c + 1) * SC_COLS]
            hi = y[:, (2 * c + 1) * SC_COLS:(2 * c + 2) * SC_COLS]
            ys_ref[...] = _pack_bf16_pair(lo, hi)


def _moe_group(xs, tile_e, tile_valid, l, W, tmx):
    n_rows = xs[0].shape[0]
    chunk = pl.BlockSpec((tmx, SC_COLS), lambda t, te, tv: (t, 0))
    grid_spec = pltpu.PrefetchScalarGridSpec(
        num_scalar_prefetch=2,
        grid=(n_rows // tmx,),
        in_specs=[chunk] * SC_CHUNKS + [
            pl.BlockSpec((None, 1, D_MODEL, 2 * D_FF), lambda t, te, tv: (l // 2, te[t], 0, 0)),
            pl.BlockSpec((None, 1, D_FF, D_MODEL), lambda t, te, tv: (l // 2, te[t], 0, 0))],
        out_specs=[chunk] * (SC_CHUNKS // 2),
    )
    return pl.pallas_call(
        _moe_group_kernel,
        grid_spec=grid_spec,
        out_shape=[jax.ShapeDtypeStruct((n_rows, SC_COLS), jnp.uint32)] * (SC_CHUNKS // 2),
        compiler_params=_params("arbitrary"),
        name="moe_group",
    )(tile_e, tile_valid, *xs, W["w_gu_moe"], W["w_down_moe"])


def _moe_combine_kernel(x_ref, rw_ref, *refs):
    y_refs, o_ref = refs[:-1], refs[-1]
    o_ref[0] = _expert_sum(x_ref[0], rw_ref[0], [y.at[0, 0] for y in y_refs])


def _expert_args(yg, rw, batch, seq_len, tm):
    args = [rw.reshape(batch, seq_len, LANES)]
    specs = [pl.BlockSpec((1, tm, LANES), lambda b, i: (b, i, 0))]
    for y in yg:
        y4 = y.reshape(2, batch, seq_len, SC_COLS)
        for choice in range(2):
            args.append(y4)
            specs.append(pl.BlockSpec((1, 1, tm, SC_COLS), lambda b, i, choice=choice: (choice, b, i, 0)))
    return args, specs


def _moe_combine(x, yg, rw, tm):
    B, S, _ = x.shape
    row = pl.BlockSpec((1, tm, D_MODEL), lambda b, i: (b, i, 0))
    args, specs = _expert_args(yg, rw, B, S, tm)
    return pl.pallas_call(
        _moe_combine_kernel,
        grid=(B, S // tm),
        in_specs=[row] + specs,
        out_specs=row,
        out_shape=jax.ShapeDtypeStruct((B, S, D_MODEL), F32),
        compiler_params=_params("parallel", "parallel"),
        name="moe_combine",
    )(x, *args)


def _moe(hf_chunks, l, W):
    n = hf_chunks[0].shape[0]
    ridx, rw = _router(hf_chunks, l, W, min(2048, n))
    dest, tile_e, tile_valid, n_rows = _route_plan(ridx, MOE_TILE)
    xs = _sc_scatter_rows(hf_chunks, dest, n_rows)
    ys = _moe_group(xs, tile_e, tile_valid, l, W, MOE_TILE)
    yg = _sc_gather_rows(ys, dest.reshape(2 * n))
    return yg, rw


def _rope_tables(seq_len):
    half = QK_ROPE // 2
    freqs = jnp.power(ROPE_THETA, -jnp.arange(half, dtype=F32) / half)
    ang = jnp.arange(seq_len).astype(F32)[:, None] * freqs[None, :]
    c, s = jnp.cos(ang), jnp.sin(ang)
    z = lambda n: jnp.zeros((seq_len, n), F32)
    o = lambda n: jnp.ones((seq_len, n), F32)
    tail = LANES - QK_NOPE - QK_ROPE
    cos = jnp.concatenate([o(QK_NOPE), c, c, o(tail)], axis=-1)
    sina = jnp.concatenate([z(QK_NOPE), -s, z(half), z(tail)], axis=-1)
    sinb = jnp.concatenate([z(QK_NOPE), z(half), s, z(tail)], axis=-1)
    return cos, sina, sinb


def _pad_last(a, width):
    return jnp.pad(a, [(0, 0)] * (a.ndim - 1) + [(0, width - a.shape[-1])])


def _cast_kernel(*refs):
    x_refs, o_ref = refs[:-1], refs[-1]
    width = x_refs[0].shape[1]
    for s, x_ref in enumerate(x_refs):
        o_ref[:, s * width:(s + 1) * width] = x_ref[...].astype(o_ref.dtype)


def _to_bf16(a):
    flat = a.reshape(-1, a.shape[-1])
    rows, cols = flat.shape
    bf16_rows = 16
    cap = CAST_BLOCK_BYTES // (4 * cols)
    tr = max(t for t in range(bf16_rows, cap + 1, bf16_rows) if rows % t == 0)
    width = cols // CAST_STREAMS
    assert width % LANES == 0, (cols, CAST_STREAMS)
    slab = lambda s: pl.BlockSpec((tr, width), lambda i: (i, s))
    out = pl.pallas_call(
        _cast_kernel,
        grid=(rows // tr,),
        in_specs=[slab(s) for s in range(CAST_STREAMS)],
        out_specs=pl.BlockSpec((tr, cols), lambda i: (i, 0)),
        out_shape=jax.ShapeDtypeStruct(flat.shape, BF16),
        compiler_params=_params("parallel"),
        name="cast_bf16",
    )(*([flat] * CAST_STREAMS))
    return out.reshape(a.shape)


def _prep_weights(g_mix, w_in, pool_w, pool_scale, g_cq, w_uq, g_ckv, w_ukv, g_qn_mla, g_kn_mla, g_mem,
                  w_mem_kv, g_qn_mem, g_kn_mem, w_br, w_out, g_ffn, w_gu_dense, w_down_dense, w_router,
                  w_gu_moe, w_down_moe):
    depth = w_in.shape[0]
    kr_lo = POOL_WIDTH + Q_LORA + KV_LORA
    kr_hi = kr_lo + QK_ROPE
    wi = w_in.astype(BF16)
    zc = lambda n: jnp.zeros((depth, D_MODEL, n), BF16)
    w_in_p = jnp.concatenate([wi[..., :kr_lo], zc(QK_NOPE), wi[..., kr_lo:kr_hi], zc(LANES - QK_NOPE - QK_ROPE),
                              wi[..., kr_hi:]], axis=-1)
    w_uq_p = _pad_last(w_uq.astype(BF16).reshape(depth, Q_LORA, MLA_HEADS, QK_DIM), LANES)
    ukv = w_ukv.astype(BF16).reshape(depth, KV_LORA, MLA_HEADS, QK_NOPE + V_DIM)
    w_uk_p = _pad_last(ukv[..., :QK_NOPE], LANES)
    w_uv_p = _pad_last(ukv[..., QK_NOPE:], LANES)
    row = lambda a: a[:, None, :]
    wr = _pad_last(w_router, LANES)
    wr_hi = wr.astype(BF16)
    return {
        "g_mix": row(g_mix), "w_in": w_in_p,
        "g_cq": row(g_cq), "w_uq": w_uq_p.reshape(depth, Q_LORA, MLA_HEADS * LANES),
        "g_ckv": row(g_ckv), "w_uk": w_uk_p.reshape(depth, KV_LORA, MLA_HEADS * LANES),
        "w_uv": w_uv_p.reshape(depth, KV_LORA, MLA_HEADS * LANES),
        "g_qn_mla": _pad_last(row(g_qn_mla), LANES), "g_kn_mla": _pad_last(row(g_kn_mla), LANES),
        "g_qn_mem": row(g_qn_mem), "g_kn_mem": row(g_kn_mem),
        "g_mem": row(g_mem), "w_mem_kv": w_mem_kv.astype(BF16),
        "pool_w": pool_w.astype(BF16), "pool_scale": row(pool_scale),
        "w_br": w_br.astype(BF16), "w_out": w_out.astype(BF16), "g_ffn": row(g_ffn),
        "w_gu_dense": _to_bf16(w_gu_dense), "w_down_dense": _to_bf16(w_down_dense),
        "w_router_hi": wr_hi, "w_router_lo": (wr - wr_hi.astype(F32)).astype(BF16),
        "w_gu_moe": _to_bf16(w_gu_moe), "w_down_moe": _to_bf16(w_down_moe),
    }


def _trunk(x, mem, W):
    B, S, _ = x.shape
    tm = min(512, S)
    tg = min(1024, S)
    tq = S if S <= 2048 else 1024
    tk = min(max(S // 2, 1024), S)
    tn = min(512, B * S)
    tables = _rope_tables(S)
    pending = None
    for l in range(DEPTH):
        if pending:
            up, q, k, v, qm, gate, x = _mix_in(x, pending, l, W, tables, tm)
        else:
            up, q, k, v, qm, gate = _mix_in(x, None, l, W, tables, tm)
        km, vm = _mem_kv(mem, l, W)
        ym = _mla_attn(q, k, v, tq, tk)
        if l % 2 == 0:
            x_mid, hf = _merge(False, up, ym, qm, km, vm, gate, x, l, W, tg)
            x = _ffn(hf.reshape(B * S, D_MODEL), x_mid.reshape(B * S, D_MODEL), l, W, tn).reshape(B, S, D_MODEL)
            pending = None
        else:
            x, *hf = _merge(True, up, ym, qm, km, vm, gate, x, l, W, tg)
            pending = _moe([h.reshape(B * S, SC_COLS) for h in hf], l, W)
    if pending:
        x = _moe_combine(x, *pending, tm)
    return x


def kernel(x_prompt, x_sample, mem_prompt, mem_sample, g_mix, w_in, pool_w, pool_scale, g_cq, w_uq, g_ckv, w_ukv, g_qn_mla, g_kn_mla, g_mem, w_mem_kv, g_qn_mem, g_kn_mem, w_br, w_out, g_ffn, w_gu_dense, w_down_dense, w_router, w_gu_moe, w_down_moe):
    W = _prep_weights(g_mix, w_in, pool_w, pool_scale, g_cq, w_uq, g_ckv, w_ukv, g_qn_mla, g_kn_mla, g_mem,
                      w_mem_kv, g_qn_mem, g_kn_mem, w_br, w_out, g_ffn, w_gu_dense, w_down_dense, w_router,
                      w_gu_moe, w_down_moe)
    y_prompt = _trunk(x_prompt, mem_prompt, W)
    y_sample = _trunk(x_sample, mem_sample, W)
    return (y_prompt, y_sample)
```

```python
import functools

import numpy as np

import jax
import jax.numpy as jnp
from jax import lax
from jax.experimental import pallas as pl
from jax.experimental.pallas import tpu as pltpu
from jax.experimental.pallas import tpu_sc as plsc

D_MODEL = 1024
DEPTH = 4
POOL_WINDOWS = (2, 4, 8, 16)
N_POOL_GROUPS = 4
POOL_GROUP_DIM = 128
POOL_WIDTH = 512
MLA_HEADS = 8
QK_NOPE = 64
QK_ROPE = 32
QK_DIM = 96
V_DIM = 64
Q_LORA = 384
KV_LORA = 256
ROPE_THETA = 10000.0
MEM_TOKENS = 256
MEM_HEADS = 4
MEM_HEAD_DIM = 128
MEM_WIDTH = 512
N_BRANCH = 3
D_FF = 2816
N_EXPERTS = 8
EPS = 1e-6
LOG2_E = 1.4426950408889634

LANES = 128
POOL_HALO = 16
POOL_ROWS = 256
VMEM_LIMIT_BYTES = 56 * 1024 * 1024
MOE_TILE = 512
FF_CHUNK = 256
SC_WINDOW = 128
SC_COLS = 256
SC_CHUNKS = D_MODEL // SC_COLS
CAST_BLOCK_BYTES = 12 * 1024 * 1024
CAST_STREAMS = 4
REPACK_ROWS = 256
MIX_ROWS = 512
MERGE_ROWS = 1024
FFN_ROWS = 512
ROUTER_ROWS = 2048
ATTN_Q_ROWS = 1024
ATTN_WHOLE_SEQ = 2048

C_POOL = 0
C_CQ = C_POOL + POOL_WIDTH
C_CKV = C_CQ + Q_LORA
C_KR = C_CKV + KV_LORA
C_QM = C_KR + LANES
C_GATE = C_QM + MEM_WIDTH
IN_PACKED = C_GATE + N_BRANCH * D_MODEL

BF16 = jnp.bfloat16
F32 = jnp.float32
NT_DIMS = (((1,), (1,)), ((), ()))


def _const_spec(shape):
    zeros = (0,) * len(shape)
    return pl.BlockSpec(shape, lambda *_: zeros, pipeline_mode=pl.Buffered(1))


def _layer_spec(l, shape):
    zeros = (0,) * len(shape)
    return pl.BlockSpec((None,) + tuple(shape), lambda *_: (l,) + zeros, pipeline_mode=pl.Buffered(1))


def _params(*sem):
    return pltpu.CompilerParams(dimension_semantics=sem, vmem_limit_bytes=VMEM_LIMIT_BYTES)


def _rms(x, width):
    return x * lax.rsqrt(jnp.sum(x * x, axis=-1, keepdims=True) * (1.0 / width) + EPS)


def _dot(a, b):
    return jnp.dot(a, b, preferred_element_type=F32)


def _mix_in_kernel(n_pending, x_ref, *refs):
    pending, refs = refs[:n_pending], refs[n_pending:]
    (gmix_ref, win_ref, gcq_ref, wuq_ref, gckv_ref, wuk_ref, wuv_ref, cos_ref, sina_ref, sinb_ref,
     gqn_ref, gkn_ref, gqm_ref, up_ref, q_ref, k_ref, v_ref, qm_ref, gate_ref) = refs[:19]
    x = x_ref[0]
    if n_pending:
        x = _expert_sum(x, pending[0][0], [y.at[0, 0] for y in pending[1:]])
        refs[19][0] = x
    hn = (_rms(x, D_MODEL) * gmix_ref[...]).astype(BF16)

    def proj(lo, hi):
        return _dot(hn, win_ref[:, lo:hi])

    cos, sina, sinb = cos_ref[...], sina_ref[...], sinb_ref[...]

    def rope(t):
        return t * cos + pltpu.roll(t, LANES - QK_ROPE // 2, 1) * sina + pltpu.roll(t, QK_ROPE // 2, 1) * sinb

    up_ref[0] = proj(C_POOL, C_CQ).astype(BF16)

    cqn = (_rms(proj(C_CQ, C_CKV), Q_LORA) * gcq_ref[...]).astype(BF16)
    qf = _dot(cqn, wuq_ref[...])
    gq = gqn_ref[...] * (QK_DIM ** -0.5 * LOG2_E)
    for h in range(MLA_HEADS):
        qh = rope(qf[:, h * LANES:(h + 1) * LANES])
        q_ref[0, h] = (_rms(qh, QK_DIM) * gq).astype(BF16)

    ckvn = (_rms(proj(C_CKV, C_KR), KV_LORA) * gckv_ref[...]).astype(BF16)
    kf = _dot(ckvn, wuk_ref[...])
    kr = rope(proj(C_KR, C_QM))
    gk = gkn_ref[...]
    for h in range(MLA_HEADS):
        kh = kf[:, h * LANES:(h + 1) * LANES] + kr
        k_ref[0, h] = (_rms(kh, QK_DIM) * gk).astype(BF16)
    vf = _dot(ckvn, wuv_ref[...])
    lane = lax.broadcasted_iota(jnp.int32, (vf.shape[0], LANES), 1)
    for h in range(MLA_HEADS):
        v_ref[0, h] = jnp.where(lane < V_DIM, vf[:, h * LANES:(h + 1) * LANES], 1.0).astype(BF16)

    qm = proj(C_QM, C_GATE)
    gm = gqm_ref[...] * (MEM_HEAD_DIM ** -0.5)
    for h in range(MEM_HEADS):
        qh = qm[:, h * LANES:(h + 1) * LANES]
        qm_ref[0, h] = (_rms(qh, MEM_HEAD_DIM) * gm).astype(BF16)

    for c in range(N_BRANCH):
        gl = proj(C_GATE + c * D_MODEL, C_GATE + (c + 1) * D_MODEL)
        gate_ref[0, :, c * D_MODEL:(c + 1) * D_MODEL] = (0.5 * jnp.tanh(0.5 * gl) + 0.5).astype(BF16)


def _mix_in(x, pending, l, W, tables, tm):
    B, S, _ = x.shape
    cos, sina, sinb = tables
    row = lambda w: pl.BlockSpec((1, tm, w), lambda b, i: (b, i, 0))
    heads = lambda n: pl.BlockSpec((1, n, tm, LANES), lambda b, i: (b, 0, i, 0))
    tab = pl.BlockSpec((tm, LANES), lambda b, i: (i, 0))
    ls = functools.partial(_layer_spec, l)
    head_shape = jax.ShapeDtypeStruct((B, MLA_HEADS, S, LANES), BF16)
    pend_args, pend_specs = _expert_args(*pending, B, S, tm) if pending else ([], [])
    out_specs = [row(POOL_WIDTH), heads(MLA_HEADS), heads(MLA_HEADS), heads(MLA_HEADS),
                 heads(MEM_HEADS), row(N_BRANCH * D_MODEL)]
    out_shape = [jax.ShapeDtypeStruct((B, S, POOL_WIDTH), BF16), head_shape, head_shape, head_shape,
                 jax.ShapeDtypeStruct((B, MEM_HEADS, S, LANES), BF16),
                 jax.ShapeDtypeStruct((B, S, N_BRANCH * D_MODEL), BF16)]
    if pending:
        out_specs.append(row(D_MODEL))
        out_shape.append(jax.ShapeDtypeStruct((B, S, D_MODEL), F32))
    return pl.pallas_call(
        functools.partial(_mix_in_kernel, len(pend_args)),
        grid=(B, S // tm),
        in_specs=[row(D_MODEL)] + pend_specs + [
            ls((1, D_MODEL)), ls((D_MODEL, IN_PACKED)),
            ls((1, Q_LORA)), ls((Q_LORA, MLA_HEADS * LANES)),
            ls((1, KV_LORA)), ls((KV_LORA, MLA_HEADS * LANES)), ls((KV_LORA, MLA_HEADS * LANES)),
            tab, tab, tab,
            ls((1, LANES)), ls((1, LANES)), ls((1, LANES))],
        out_specs=out_specs,
        out_shape=out_shape,
        compiler_params=_params("parallel", "parallel"),
        name="mix_in",
    )(x, *pend_args, W["g_mix"], W["w_in"], W["g_cq"], W["w_uq"], W["g_ckv"], W["w_uk"], W["w_uv"],
      cos, sina, sinb, W["g_qn_mla"], W["g_kn_mla"], W["g_qn_mem"])


def _mem_kv_kernel(mem_ref, gmem_ref, w_ref, gkn_ref, km_ref, vm_ref):
    hn = (_rms(mem_ref[0], D_MODEL) * gmem_ref[...]).astype(BF16)
    kv = _dot(hn, w_ref[...])
    for h in range(MEM_HEADS):
        kh = kv[:, h * LANES:(h + 1) * LANES]
        km_ref[0, h] = (_rms(kh, MEM_HEAD_DIM) * gkn_ref[...]).astype(BF16)
        vm_ref[0, h] = kv[:, MEM_WIDTH + h * LANES:MEM_WIDTH + (h + 1) * LANES].astype(BF16)


def _mem_kv(mem, l, W):
    B, M, _ = mem.shape
    out = pl.BlockSpec((1, MEM_HEADS, M, LANES), lambda b: (b, 0, 0, 0))
    shape = jax.ShapeDtypeStruct((B, MEM_HEADS, M, LANES), BF16)
    return pl.pallas_call(
        _mem_kv_kernel,
        grid=(B,),
        in_specs=[pl.BlockSpec((1, M, D_MODEL), lambda b: (b, 0, 0)), _layer_spec(l, (1, D_MODEL)),
                  _layer_spec(l, (D_MODEL, 2 * MEM_WIDTH)), _layer_spec(l, (1, LANES))],
        out_specs=[out, out],
        out_shape=[shape, shape],
        compiler_params=_params("parallel"),
        name="mem_kv",
    )(mem, W["g_mem"], W["w_mem_kv"], W["g_kn_mem"])


def _pool_bands():
    i = np.arange(POOL_ROWS)[:, None]
    r = np.arange(-POOL_HALO, POOL_ROWS + POOL_HALO)[None, :]
    bands = [((r - i >= -(w // 2)) & (r - i < w // 2)).astype(np.float32) for w in POOL_WINDOWS]
    return jnp.asarray(np.stack(bands), BF16)


def _pool_mix(seq_len, row0, has_prev, has_next, cur_ref, prev_ref, next_ref, band_ref, pw_ref, ps_ref):
    tm = cur_ref.shape[1]
    prev = jnp.where(has_prev, prev_ref[0], jnp.zeros_like(prev_ref[0]))
    nxt = jnp.where(has_next, next_ref[0], jnp.zeros_like(next_ref[0]))
    ext = jnp.concatenate([prev, cur_ref[0], nxt], axis=0)
    out = []
    for r in range(0, tm, POOL_ROWS):
        win = ext[r:r + POOL_ROWS + 2 * POOL_HALO]
        t = row0 + r + lax.broadcasted_iota(jnp.int32, (POOL_ROWS, 1), 0)
        groups = []
        for g, w in enumerate(POOL_WINDOWS):
            sl = slice(g * POOL_GROUP_DIM, (g + 1) * POOL_GROUP_DIM)
            count = jnp.minimum(t + w // 2, seq_len) - jnp.maximum(t - w // 2, 0)
            u = win[POOL_HALO:POOL_HALO + POOL_ROWS, sl].astype(F32)
            d = _dot(band_ref[g], win[:, sl]) / count.astype(F32) - u
            groups.append(_dot(d.astype(BF16), pw_ref[g]) * ps_ref[:, sl])
        out.append(jnp.concatenate(groups, axis=-1).astype(BF16))
    return jnp.concatenate(out, axis=0)


def _softmax_pv(s, v):
    m = jnp.max(s, axis=-1, keepdims=True)
    p = jnp.exp(s - m)
    l = jnp.sum(p, axis=-1, keepdims=True)
    return _dot(p.astype(BF16), v) / l


def _mla_attn_kernel(seq_len, tk, q_ref, k_ref, v_ref, *refs):
    if len(refs) == 3:
        slab_ref, o_ref, slab_out_ref = refs
        slab_out_ref[...] = slab_ref[...].astype(BF16)
    else:
        (o_ref,) = refs
    outs = []
    for hh in range(2):
        q = q_ref[0, hh]
        m = acc = None
        for c in range(seq_len // tk):
            ks = slice(c * tk, (c + 1) * tk)
            s = lax.dot_general(q, k_ref[0, hh, ks, :], NT_DIMS, preferred_element_type=F32)
            mc = jnp.max(s, axis=-1, keepdims=True)
            m_new = mc if m is None else jnp.maximum(m, mc)
            pv = _dot(jnp.exp2(s - m_new).astype(BF16), v_ref[0, hh, ks, :])
            acc = pv if acc is None else acc * jnp.exp2(m - m_new) + pv
            m = m_new
        outs.append(acc * (1.0 / acc[:, V_DIM:V_DIM + 1]))
    lane = lax.broadcasted_iota(jnp.int32, outs[0].shape, 1)
    o_ref[0] = jnp.where(lane < V_DIM, outs[0], pltpu.roll(outs[1], V_DIM, 1)).astype(BF16)


def _mla_attn(q, k, v, tq, tk, passenger=None):
    B, H, S, _ = q.shape
    grid = (B, H // 2, S // tq)
    kv = pl.BlockSpec((1, 2, S, LANES), lambda b, p, i: (b, p, 0, 0))
    in_specs = [pl.BlockSpec((1, 2, tq, LANES), lambda b, p, i: (b, p, i, 0)), kv, kv]
    out_specs = [pl.BlockSpec((1, tq, 2 * V_DIM), lambda b, p, i: (b, i, p))]
    out_shape = [jax.ShapeDtypeStruct((B, S, H * V_DIM), BF16)]
    args = [q, k, v]
    if passenger is not None:
        flat = passenger.reshape(-1, passenger.shape[-1])
        steps = grid[0] * grid[1] * grid[2]
        rows = flat.shape[0] // steps
        assert rows * steps == flat.shape[0] and rows % 16 == 0, (flat.shape, steps)
        slab = pl.BlockSpec((rows, flat.shape[1]), lambda b, p, i: ((b * grid[1] + p) * grid[2] + i, 0))
        in_specs.append(slab)
        out_specs.append(slab)
        out_shape.append(jax.ShapeDtypeStruct(flat.shape, BF16))
        args.append(flat)
    outs = pl.pallas_call(
        functools.partial(_mla_attn_kernel, S, tk),
        grid=grid,
        in_specs=in_specs, out_specs=out_specs, out_shape=out_shape,
        compiler_params=_params("parallel", "parallel", "arbitrary"),
        name="mla_attn",
    )(*args)
    if passenger is None:
        return outs[0]
    return outs[0], outs[1].reshape(passenger.shape)


def _merge_kernel(is_moe, seq_len, up_ref, prev_ref, next_ref, band_ref, pw_ref, ps_ref,
                  ym_ref, qm_ref, km_ref, vm_ref, gate_ref, wbr_ref, wout_ref,
                  x_ref, gffn_ref, xo_ref, *hf_refs):
    i = pl.program_id(1)
    y_pool = _pool_mix(seq_len, i * up_ref.shape[1], i > 0, i < pl.num_programs(1) - 1,
                       up_ref, prev_ref, next_ref, band_ref, pw_ref, ps_ref)

    mem = []
    for h in range(MEM_HEADS):
        s = lax.dot_general(qm_ref[0, h], km_ref[0, h], NT_DIMS, preferred_element_type=F32)
        mem.append(_softmax_pv(s, vm_ref[0, h]).astype(BF16))
    y_mem = jnp.concatenate(mem, axis=-1)

    branches = (y_pool, ym_ref[0], y_mem)
    merged = None
    for c, y in enumerate(branches):
        term = gate_ref[0, :, c * D_MODEL:(c + 1) * D_MODEL].astype(F32) * _dot(y, wbr_ref[c])
        merged = term if merged is None else merged + term
    x_new = x_ref[0] + _dot(merged.astype(BF16), wout_ref[...])
    xo_ref[0] = x_new
    hf = _rms(x_new, D_MODEL) * gffn_ref[...]
    if is_moe:
        for c, ref in enumerate(hf_refs):
            ref[0] = hf[:, c * SC_COLS:(c + 1) * SC_COLS]
    else:
        hf_refs[0][0] = hf.astype(BF16)


def _merge(is_moe, up, ym, qm, km, vm, gate, x, l, W, tm):
    B, S, _ = x.shape
    M = km.shape[2]
    row = lambda w: pl.BlockSpec((1, tm, w), lambda b, i: (b, i, 0))
    memkv = pl.BlockSpec((1, MEM_HEADS, M, LANES), lambda b, i: (b, 0, 0, 0))
    per = tm // POOL_HALO
    last = S // POOL_HALO - 1
    halo = lambda index: pl.BlockSpec((1, POOL_HALO, POOL_WIDTH), lambda b, i: (b, index(i), 0))
    in_specs = [row(POOL_WIDTH), halo(lambda i: jnp.maximum(i * per - 1, 0)),
                halo(lambda i: jnp.minimum((i + 1) * per, last)),
                _const_spec((N_POOL_GROUPS, POOL_ROWS, POOL_ROWS + 2 * POOL_HALO)),
                _layer_spec(l, (N_POOL_GROUPS, POOL_GROUP_DIM, POOL_GROUP_DIM)), _layer_spec(l, (1, POOL_WIDTH)),
                row(MLA_HEADS * V_DIM),
                pl.BlockSpec((1, MEM_HEADS, tm, LANES), lambda b, i: (b, 0, i, 0)), memkv, memkv,
                row(N_BRANCH * D_MODEL), _layer_spec(l, (N_BRANCH, POOL_WIDTH, D_MODEL)),
                _layer_spec(l, (D_MODEL, D_MODEL)), row(D_MODEL), _layer_spec(l, (1, D_MODEL))]
    out_specs = [row(D_MODEL)]
    out_shape = [jax.ShapeDtypeStruct((B, S, D_MODEL), F32)]
    if is_moe:
        out_specs += [row(SC_COLS)] * SC_CHUNKS
        out_shape += [jax.ShapeDtypeStruct((B, S, SC_COLS), F32)] * SC_CHUNKS
    else:
        out_specs.append(row(D_MODEL))
        out_shape.append(jax.ShapeDtypeStruct((B, S, D_MODEL), BF16))
    return pl.pallas_call(
        functools.partial(_merge_kernel, is_moe, S),
        grid=(B, S // tm),
        in_specs=in_specs, out_specs=out_specs, out_shape=out_shape,
        compiler_params=_params("parallel", "parallel"),
        name="merge_moe" if is_moe else "merge",
    )(up, up, up, _pool_bands(), W["pool_w"], W["pool_scale"],
      ym, qm, km, vm, gate, W["w_br"], W["w_out"], x, W["g_ffn"])


def _swiglu_chunk(h, wg, wu):
    g = _dot(h, wg)
    return g * jax.nn.sigmoid(g) * _dot(h, wu)


def _swiglu_down(h, wgu_ref, wd_ref, acc):
    for j in range(D_FF // FF_CHUNK):
        lo, hi = j * FF_CHUNK, (j + 1) * FF_CHUNK
        a = _swiglu_chunk(h, wgu_ref[:, lo:hi], wgu_ref[:, D_FF + lo:D_FF + hi])
        acc = acc + _dot(a.astype(BF16), wd_ref[lo:hi, :])
    return acc


def _ffn_kernel(hf_ref, wgu_ref, wd_ref, x_ref, o_ref):
    o_ref[...] = _swiglu_down(hf_ref[...], wgu_ref, wd_ref, x_ref[...])


def _ffn(hf, x, l, W, tm):
    N = hf.shape[0]
    row = pl.BlockSpec((tm, D_MODEL), lambda i: (i, 0))
    return pl.pallas_call(
        _ffn_kernel,
        grid=(N // tm,),
        in_specs=[row, _layer_spec(l // 2, (D_MODEL, 2 * D_FF)), _layer_spec(l // 2, (D_FF, D_MODEL)), row],
        out_specs=row,
        out_shape=jax.ShapeDtypeStruct((N, D_MODEL), F32),
        compiler_params=_params("parallel"),
        name="ffn",
    )(hf, W["w_gu_dense"], W["w_down_dense"], x)


def _router_kernel(*refs):
    hf_refs, (wrh_ref, wrl_ref, ridx_ref, rw_ref) = refs[:SC_CHUNKS], refs[SC_CHUNKS:]
    logits = None
    for c, hf_ref in enumerate(hf_refs):
        rows = slice(c * SC_COLS, (c + 1) * SC_COLS)
        hf = hf_ref[...]
        hi = hf.astype(BF16)
        lo = (hf - hi.astype(F32)).astype(BF16)
        part = _dot(hi, wrh_ref[rows, :]) + _dot(lo, wrh_ref[rows, :]) + _dot(hi, wrl_ref[rows, :])
        logits = part if logits is None else logits + part
    lane = lax.broadcasted_iota(jnp.int32, logits.shape, 1)
    lg = jnp.where(lane < N_EXPERTS, logits, -jnp.inf)
    m1 = jnp.max(lg, axis=-1, keepdims=True)
    i1 = jnp.min(jnp.where(lg == m1, lane, LANES), axis=-1, keepdims=True)
    lg2 = jnp.where(lane == i1, -jnp.inf, lg)
    m2 = jnp.max(lg2, axis=-1, keepdims=True)
    i2 = jnp.min(jnp.where(lg2 == m2, lane, LANES), axis=-1, keepdims=True)
    e = jnp.exp(m2 - m1)
    w1 = 1.0 / (1.0 + e)
    ridx_ref[...] = jnp.where(lane == 0, i1, jnp.where(lane == 1, i2, 0))
    rw_ref[...] = jnp.where(lane == 0, w1, jnp.where(lane == 1, e * w1, 0.0))


def _router(hf_chunks, l, W, tr):
    n = hf_chunks[0].shape[0]
    out = pl.BlockSpec((tr, LANES), lambda i: (i, 0))
    return pl.pallas_call(
        _router_kernel,
        grid=(n // tr,),
        in_specs=[pl.BlockSpec((tr, SC_COLS), lambda i: (i, 0))] * SC_CHUNKS
                 + [_layer_spec(l // 2, (D_MODEL, LANES)), _layer_spec(l // 2, (D_MODEL, LANES))],
        out_specs=[out, out],
        out_shape=[jax.ShapeDtypeStruct((n, LANES), jnp.int32), jax.ShapeDtypeStruct((n, LANES), F32)],
        compiler_params=_params("parallel"),
        name="router",
    )(*hf_chunks, W["w_router_hi"], W["w_router_lo"])


def _route_plan(ridx, tmx):
    n = ridx.shape[0]
    e = ridx[:, :2].T.reshape(2 * n)
    onehot = (e[:, None] == jnp.arange(N_EXPERTS, dtype=jnp.int32)[None, :]).astype(jnp.int32)
    csum = jnp.cumsum(onehot, axis=0)
    rank = jnp.sum((csum - onehot) * onehot, axis=-1)
    counts = csum[-1]
    padded = ((counts + tmx - 1) // tmx) * tmx
    ends = jnp.cumsum(padded)
    starts = ends - padded
    dest = jnp.sum(onehot * starts[None, :], axis=-1) + rank
    n_tiles = (2 * n) // tmx + N_EXPERTS
    tile_row0 = jnp.arange(n_tiles, dtype=jnp.int32) * tmx
    tile_e = jnp.minimum(jnp.sum((tile_row0[:, None] >= ends[None, :]).astype(jnp.int32), axis=-1),
                         N_EXPERTS - 1)
    group_end = jnp.take(starts + counts, tile_e)
    tile_valid = jnp.where(tile_row0 < ends[-1], jnp.clip(group_end - tile_row0, 0, tmx), 0)
    return dest.reshape(2, n).astype(jnp.int32), tile_e.astype(jnp.int32), tile_valid.astype(jnp.int32), n_tiles * tmx


def _sc_mesh():
    return plsc.VectorSubcoreMesh(core_axis_name="core", subcore_axis_name="subcore")


def _sc_scatter_rows(chunks, dest, n_out):
    n, width = chunks[0].shape
    per_choice = n // SC_WINDOW
    out_type = [jax.ShapeDtypeStruct((n_out, width), chunks[0].dtype)] * len(chunks)

    @pl.kernel(out_type=out_type, mesh=_sc_mesh(), scratch_types=[])
    def scatter(*refs):
        x_refs, i_hbm, o_refs = refs[:len(chunks)], refs[len(chunks)], refs[len(chunks) + 1:]
        for x_hbm, o_hbm in zip(x_refs, o_refs):
            def body(x_vmem, i_vmem, o_hbm=o_hbm):
                pltpu.sync_copy(x_vmem, o_hbm.at[i_vmem.at[0]])

            pltpu.emit_pipeline(
                body,
                grid=(2 * per_choice,),
                in_specs=[pl.BlockSpec((SC_WINDOW, width), lambda i: (i % per_choice, 0)),
                          pl.BlockSpec((1, SC_WINDOW), lambda i: (0, i))],
                out_specs=[],
                core_axis_name=("core", "subcore"),
                dimension_semantics=(pltpu.PARALLEL,),
            )(x_hbm, i_hbm)

    return scatter(*chunks, dest.reshape(1, 2 * n))


def _sc_gather_rows(chunks, idx):
    n = idx.shape[0]
    width = chunks[0].shape[1]
    out_type = [jax.ShapeDtypeStruct((n, width), chunks[0].dtype)] * len(chunks)

    @pl.kernel(out_type=out_type, mesh=_sc_mesh(), scratch_types=[])
    def gather(*refs):
        x_refs, i_hbm, o_refs = refs[:len(chunks)], refs[len(chunks)], refs[len(chunks) + 1:]
        for x_hbm, o_hbm in zip(x_refs, o_refs):
            def body(i_vmem, o_vmem, x_hbm=x_hbm):
                pltpu.sync_copy(x_hbm.at[i_vmem.at[0]], o_vmem)

            pltpu.emit_pipeline(
                body,
                grid=(n // SC_WINDOW,),
                in_specs=[pl.BlockSpec((1, SC_WINDOW), lambda i: (0, i))],
                out_specs=[pl.BlockSpec((SC_WINDOW, width), lambda i: (i, 0))],
                core_axis_name=("core", "subcore"),
                dimension_semantics=(pltpu.PARALLEL,),
            )(i_hbm, o_hbm)

    return gather(*chunks, idx.reshape(1, n))


def _pack_bf16_pair(lo, hi):
    bits = lambda a: lax.bitcast_convert_type(a.astype(BF16).astype(F32), jnp.uint32)
    return (bits(lo) >> 16) | (bits(hi) & jnp.uint32(0xFFFF0000))


def _unpack_bf16_pair(packed):
    f32 = lambda a: lax.bitcast_convert_type(a, F32)
    return f32(packed << 16), f32(packed & jnp.uint32(0xFFFF0000))


def _expert_sum(x, rw, y_refs):
    w0, w1 = rw[:, 0:1], rw[:, 1:2]
    cols = []
    for c in range(SC_CHUNKS // 2):
        lo0, hi0 = _unpack_bf16_pair(y_refs[2 * c][...])
        lo1, hi1 = _unpack_bf16_pair(y_refs[2 * c + 1][...])
        cols += [w0 * lo0 + w1 * lo1, w0 * hi0 + w1 * hi1]
    return x + jnp.concatenate(cols, axis=-1)


def _moe_group_kernel(te_ref, tv_ref, *refs):
    xs_refs, (wgu_ref, wd_ref) = refs[:SC_CHUNKS], refs[SC_CHUNKS:SC_CHUNKS + 2]
    ys_refs = refs[SC_CHUNKS + 2:]
    valid = tv_ref[pl.program_id(0)]

    @pl.when(valid > 0)
    def _():
        row = lax.broadcasted_iota(jnp.int32, xs_refs[0].shape, 0)
        h = jnp.concatenate([jnp.where(row < valid, xs_ref[...], 0.0).astype(BF16) for xs_ref in xs_refs], axis=-1)
        y = _swiglu_down(h, wgu_ref.at[0], wd_ref.at[0], jnp.zeros((h.shape[0], D_MODEL), F32))
        for c, ys_ref in enumerate(ys_refs):
            lo = y[:, 2 * c * SC_COLS:(2 * c + 1) * SC_COLS]
            hi = y[:, (2 * c + 1) * SC_COLS:(2 * c + 2) * SC_COLS]
            ys_ref[...] = _pack_bf16_pair(lo, hi)


def _moe_group(xs, tile_e, tile_valid, l, W, tmx):
    n_rows = xs[0].shape[0]
    chunk = pl.BlockSpec((tmx, SC_COLS), lambda t, te, tv: (t, 0))
    grid_spec = pltpu.PrefetchScalarGridSpec(
        num_scalar_prefetch=2,
        grid=(n_rows // tmx,),
        in_specs=[chunk] * SC_CHUNKS + [
            pl.BlockSpec((None, 1, D_MODEL, 2 * D_FF), lambda t, te, tv: (l // 2, te[t], 0, 0)),
            pl.BlockSpec((None, 1, D_FF, D_MODEL), lambda t, te, tv: (l // 2, te[t], 0, 0))],
        out_specs=[chunk] * (SC_CHUNKS // 2),
    )
    return pl.pallas_call(
        _moe_group_kernel,
        grid_spec=grid_spec,
        out_shape=[jax.ShapeDtypeStruct((n_rows, SC_COLS), jnp.uint32)] * (SC_CHUNKS // 2),
        compiler_params=_params("arbitrary"),
        name="moe_group",
    )(tile_e, tile_valid, *xs, W["w_gu_moe"], W["w_down_moe"])


def _moe_combine_kernel(x_ref, rw_ref, *refs):
    y_refs, o_ref = refs[:-1], refs[-1]
    o_ref[0] = _expert_sum(x_ref[0], rw_ref[0], [y.at[0, 0] for y in y_refs])


def _expert_args(yg, rw, batch, seq_len, tm):
    args = [rw.reshape(batch, seq_len, LANES)]
    specs = [pl.BlockSpec((1, tm, LANES), lambda b, i: (b, i, 0))]
    for y in yg:
        y4 = y.reshape(2, batch, seq_len, SC_COLS)
        for choice in range(2):
            args.append(y4)
            specs.append(pl.BlockSpec((1, 1, tm, SC_COLS), lambda b, i, choice=choice: (choice, b, i, 0)))
    return args, specs


def _moe_combine(x, yg, rw, tm):
    B, S, _ = x.shape
    row = pl.BlockSpec((1, tm, D_MODEL), lambda b, i: (b, i, 0))
    args, specs = _expert_args(yg, rw, B, S, tm)
    return pl.pallas_call(
        _moe_combine_kernel,
        grid=(B, S // tm),
        in_specs=[row] + specs,
        out_specs=row,
        out_shape=jax.ShapeDtypeStruct((B, S, D_MODEL), F32),
        compiler_params=_params("parallel", "parallel"),
        name="moe_combine",
    )(x, *args)


def _moe(hf_chunks, l, W):
    n = hf_chunks[0].shape[0]
    ridx, rw = _router(hf_chunks, l, W, min(ROUTER_ROWS, n))
    dest, tile_e, tile_valid, n_rows = _route_plan(ridx, MOE_TILE)
    xs = _sc_scatter_rows(hf_chunks, dest, n_rows)
    ys = _moe_group(xs, tile_e, tile_valid, l, W, MOE_TILE)
    yg = _sc_gather_rows(ys, dest.reshape(2 * n))
    return yg, rw


def _rope_tables(seq_len):
    half = QK_ROPE // 2
    freqs = jnp.power(ROPE_THETA, -jnp.arange(half, dtype=F32) / half)
    ang = jnp.arange(seq_len).astype(F32)[:, None] * freqs[None, :]
    c, s = jnp.cos(ang), jnp.sin(ang)
    z = lambda n: jnp.zeros((seq_len, n), F32)
    o = lambda n: jnp.ones((seq_len, n), F32)
    tail = LANES - QK_NOPE - QK_ROPE
    cos = jnp.concatenate([o(QK_NOPE), c, c, o(tail)], axis=-1)
    sina = jnp.concatenate([z(QK_NOPE), -s, z(half), z(tail)], axis=-1)
    sinb = jnp.concatenate([z(QK_NOPE), z(half), s, z(tail)], axis=-1)
    return cos, sina, sinb


def _pad_last(a, width):
    return jnp.pad(a, [(0, 0)] * (a.ndim - 1) + [(0, width - a.shape[-1])])


def _cast_kernel(*refs):
    x_refs, o_ref = refs[:-1], refs[-1]
    width = x_refs[0].shape[1]
    for s, x_ref in enumerate(x_refs):
        o_ref[:, s * width:(s + 1) * width] = x_ref[...].astype(o_ref.dtype)


def _to_bf16(a):
    flat = a.reshape(-1, a.shape[-1])
    rows, cols = flat.shape
    bf16_rows = 16
    cap = CAST_BLOCK_BYTES // (4 * cols)
    tr = max(t for t in range(bf16_rows, cap + 1, bf16_rows) if rows % t == 0)
    width = cols // CAST_STREAMS
    assert width % LANES == 0, (cols, CAST_STREAMS)
    slab = lambda s: pl.BlockSpec((tr, width), lambda i: (i, s))
    out = pl.pallas_call(
        _cast_kernel,
        grid=(rows // tr,),
        in_specs=[slab(s) for s in range(CAST_STREAMS)],
        out_specs=pl.BlockSpec((tr, cols), lambda i: (i, 0)),
        out_shape=jax.ShapeDtypeStruct(flat.shape, BF16),
        compiler_params=_params("parallel"),
        name="cast_bf16",
    )(*([flat] * CAST_STREAMS))
    return out.reshape(a.shape)


def _repack_w_in_kernel(w_ref, o_ref):
    kr_lo = POOL_WIDTH + Q_LORA + KV_LORA
    kr_hi = kr_lo + QK_ROPE
    w = w_ref[...]
    o_ref[:, :C_KR] = w[:, :kr_lo].astype(BF16)
    head = w[:, kr_lo:kr_lo + LANES]
    lane = lax.broadcasted_iota(jnp.int32, head.shape, 1)
    in_slot = (lane >= QK_NOPE) & (lane < QK_NOPE + QK_ROPE)
    o_ref[:, C_KR:C_QM] = jnp.where(in_slot, pltpu.roll(head, QK_NOPE, 1), 0.0).astype(BF16)
    o_ref[:, C_QM:] = w[:, kr_hi:].astype(BF16)


def _repack_w_in(w_in):
    depth = w_in.shape[0]
    flat = w_in.reshape(depth * D_MODEL, w_in.shape[-1])
    out = pl.pallas_call(
        _repack_w_in_kernel,
        grid=(flat.shape[0] // REPACK_ROWS,),
        in_specs=[pl.BlockSpec((REPACK_ROWS, flat.shape[1]), lambda i: (i, 0))],
        out_specs=pl.BlockSpec((REPACK_ROWS, IN_PACKED), lambda i: (i, 0)),
        out_shape=jax.ShapeDtypeStruct((flat.shape[0], IN_PACKED), BF16),
        compiler_params=_params("parallel"),
        name="repack_w_in",
    )(flat)
    return out.reshape(depth, D_MODEL, IN_PACKED)


def _prep_weights(g_mix, w_in, pool_w, pool_scale, g_cq, w_uq, g_ckv, w_ukv, g_qn_mla, g_kn_mla, g_mem,
                  w_mem_kv, g_qn_mem, g_kn_mem, w_br, w_out, g_ffn, w_gu_dense, w_down_dense, w_router):
    depth = w_in.shape[0]
    w_in_p = _repack_w_in(w_in)
    w_uq_p = _pad_last(w_uq.astype(BF16).reshape(depth, Q_LORA, MLA_HEADS, QK_DIM), LANES)
    ukv = w_ukv.astype(BF16).reshape(depth, KV_LORA, MLA_HEADS, QK_NOPE + V_DIM)
    w_uk_p = _pad_last(ukv[..., :QK_NOPE], LANES)
    w_uv_p = _pad_last(ukv[..., QK_NOPE:], LANES)
    row = lambda a: a[:, None, :]
    wr = _pad_last(w_router, LANES)
    wr_hi = wr.astype(BF16)
    return {
        "g_mix": row(g_mix), "w_in": w_in_p,
        "g_cq": row(g_cq), "w_uq": w_uq_p.reshape(depth, Q_LORA, MLA_HEADS * LANES),
        "g_ckv": row(g_ckv), "w_uk": w_uk_p.reshape(depth, KV_LORA, MLA_HEADS * LANES),
        "w_uv": w_uv_p.reshape(depth, KV_LORA, MLA_HEADS * LANES),
        "g_qn_mla": _pad_last(row(g_qn_mla), LANES), "g_kn_mla": _pad_last(row(g_kn_mla), LANES),
        "g_qn_mem": row(g_qn_mem), "g_kn_mem": row(g_kn_mem),
        "g_mem": row(g_mem), "w_mem_kv": w_mem_kv.astype(BF16),
        "pool_w": pool_w.astype(BF16), "pool_scale": row(pool_scale),
        "w_br": w_br.astype(BF16), "w_out": w_out.astype(BF16), "g_ffn": row(g_ffn),
        "w_gu_dense": _to_bf16(w_gu_dense), "w_down_dense": _to_bf16(w_down_dense),
        "w_router_hi": wr_hi, "w_router_lo": (wr - wr_hi.astype(F32)).astype(BF16),
    }


def _layer(x, pending, mem, l, W, tables, passenger=None):
    B, S, _ = x.shape
    tm = min(MIX_ROWS, S)
    tg = min(MERGE_ROWS, S)
    tq = S if S <= ATTN_WHOLE_SEQ else ATTN_Q_ROWS
    tk = min(max(S // 2, 1024), S)
    tn = min(FFN_ROWS, B * S)
    if pending:
        up, q, k, v, qm, gate, x = _mix_in(x, pending, l, W, tables, tm)
    else:
        up, q, k, v, qm, gate = _mix_in(x, None, l, W, tables, tm)
    km, vm = _mem_kv(mem, l, W)
    cast = None
    if passenger is None:
        ym = _mla_attn(q, k, v, tq, tk)
    else:
        ym, cast = _mla_attn(q, k, v, tq, tk, passenger)
    if l % 2 == 0:
        x_mid, hf = _merge(False, up, ym, qm, km, vm, gate, x, l, W, tg)
        x = _ffn(hf.reshape(B * S, D_MODEL), x_mid.reshape(B * S, D_MODEL), l, W, tn).reshape(B, S, D_MODEL)
        return x, None, cast
    x, *hf = _merge(True, up, ym, qm, km, vm, gate, x, l, W, tg)
    return x, _moe([h.reshape(B * S, SC_COLS) for h in hf], l, W), cast


def kernel(x_prompt, x_sample, mem_prompt, mem_sample, g_mix, w_in, pool_w, pool_scale, g_cq, w_uq, g_ckv, w_ukv, g_qn_mla, g_kn_mla, g_mem, w_mem_kv, g_qn_mem, g_kn_mem, w_br, w_out, g_ffn, w_gu_dense, w_down_dense, w_router, w_gu_moe, w_down_moe):
    W = _prep_weights(g_mix, w_in, pool_w, pool_scale, g_cq, w_uq, g_ckv, w_ukv, g_qn_mla, g_kn_mla, g_mem,
                      w_mem_kv, g_qn_mem, g_kn_mem, w_br, w_out, g_ffn, w_gu_dense, w_down_dense, w_router)
    xs = [x_prompt, x_sample]
    mems = [mem_prompt, mem_sample]
    tables = [_rope_tables(x.shape[1]) for x in xs]
    pending = [None, None]
    passengers = [("w_gu_moe", w_gu_moe), ("w_down_moe", w_down_moe)]
    for l in range(DEPTH):
        for t in range(2):
            name, weight = passengers[t] if l == 0 else (None, None)
            xs[t], pending[t], cast = _layer(xs[t], pending[t], mems[t], l, W, tables[t], weight)
            if name:
                W[name] = cast
    for t in range(2):
        if pending[t]:
            xs[t] = _moe_combine(xs[t], *pending[t], min(MIX_ROWS, xs[t].shape[1]))
    return tuple(xs)
```

```python
import functools

import numpy as np

import jax
import jax.numpy as jnp
from jax import lax
from jax.experimental import pallas as pl
from jax.experimental.pallas import tpu as pltpu
from jax.experimental.pallas import tpu_sc as plsc

D_MODEL = 1024
DEPTH = 4
POOL_WINDOWS = (2, 4, 8, 16)
N_POOL_GROUPS = 4
POOL_GROUP_DIM = 128
POOL_WIDTH = 512
MLA_HEADS = 8
QK_NOPE = 64
QK_ROPE = 32
QK_DIM = 96
V_DIM = 64
Q_LORA = 384
KV_LORA = 256
ROPE_THETA = 10000.0
MEM_TOKENS = 256
MEM_HEADS = 4
MEM_HEAD_DIM = 128
MEM_WIDTH = 512
N_BRANCH = 3
D_FF = 2816
N_EXPERTS = 8
EPS = 1e-6
LOG2_E = 1.4426950408889634

LANES = 128
POOL_HALO = 16
POOL_ROWS = 256
VMEM_LIMIT_BYTES = 56 * 1024 * 1024
MOE_TILE = 512
FF_CHUNK = 256
SC_WINDOW = 128
SC_COLS = 256
SC_CHUNKS = D_MODEL // SC_COLS
CAST_BLOCK_BYTES = 12 * 1024 * 1024
CAST_STREAMS = 4
REPACK_ROWS = 256
MIX_ROWS = 512
MERGE_ROWS = 1024
FFN_ROWS = 512
ROUTER_ROWS = 2048
ATTN_Q_ROWS = 1024
ATTN_WHOLE_SEQ = 2048

C_POOL = 0
C_CQ = C_POOL + POOL_WIDTH
C_CKV = C_CQ + Q_LORA
C_KR = C_CKV + KV_LORA
C_QM = C_KR + LANES
C_GATE = C_QM + MEM_WIDTH
IN_PACKED = C_GATE + N_BRANCH * D_MODEL

BF16 = jnp.bfloat16
F32 = jnp.float32
NT_DIMS = (((1,), (1,)), ((), ()))


def _const_spec(shape):
    zeros = (0,) * len(shape)
    return pl.BlockSpec(shape, lambda *_: zeros, pipeline_mode=pl.Buffered(1))


def _layer_spec(l, shape):
    zeros = (0,) * len(shape)
    return pl.BlockSpec((None,) + tuple(shape), lambda *_: (l,) + zeros, pipeline_mode=pl.Buffered(1))


def _params(*sem):
    return pltpu.CompilerParams(dimension_semantics=sem, vmem_limit_bytes=VMEM_LIMIT_BYTES)


def _rms(x, width):
    return x * lax.rsqrt(jnp.sum(x * x, axis=-1, keepdims=True) * (1.0 / width) + EPS)


def _dot(a, b):
    return jnp.dot(a, b, preferred_element_type=F32)


def _mix_in_kernel(n_pending, x_ref, *refs):
    pending, refs = refs[:n_pending], refs[n_pending:]
    (gmix_ref, win_ref, gcq_ref, wuq_ref, gckv_ref, wuk_ref, wuv_ref, cos_ref, sina_ref, sinb_ref,
     gqn_ref, gkn_ref, gqm_ref, up_ref, q_ref, k_ref, v_ref, qm_ref, gate_ref) = refs[:19]
    x = x_ref[0]
    if n_pending:
        x = _expert_sum(x, pending[0][0], [y.at[0, 0] for y in pending[1:]])
        refs[19][0] = x
    hn = (_rms(x, D_MODEL) * gmix_ref[...]).astype(BF16)

    def proj(lo, hi):
        return _dot(hn, win_ref[:, lo:hi])

    cos, sina, sinb = cos_ref[...], sina_ref[...], sinb_ref[...]

    def rope(t):
        return t * cos + pltpu.roll(t, LANES - QK_ROPE // 2, 1) * sina + pltpu.roll(t, QK_ROPE // 2, 1) * sinb

    up_ref[0] = proj(C_POOL, C_CQ).astype(BF16)

    cqn = (_rms(proj(C_CQ, C_CKV), Q_LORA) * gcq_ref[...]).astype(BF16)
    qf = _dot(cqn, wuq_ref[...])
    gq = gqn_ref[...] * (QK_DIM ** -0.5 * LOG2_E)
    for h in range(MLA_HEADS):
        qh = rope(qf[:, h * LANES:(h + 1) * LANES])
        q_ref[0, h] = (_rms(qh, QK_DIM) * gq).astype(BF16)

    ckvn = (_rms(proj(C_CKV, C_KR), KV_LORA) * gckv_ref[...]).astype(BF16)
    kf = _dot(ckvn, wuk_ref[...])
    kr = rope(proj(C_KR, C_QM))
    gk = gkn_ref[...]
    for h in range(MLA_HEADS):
        kh = kf[:, h * LANES:(h + 1) * LANES] + kr
        k_ref[0, h] = (_rms(kh, QK_DIM) * gk).astype(BF16)
    vf = _dot(ckvn, wuv_ref[...])
    lane = lax.broadcasted_iota(jnp.int32, (vf.shape[0], LANES), 1)
    for h in range(MLA_HEADS):
        v_ref[0, h] = jnp.where(lane < V_DIM, vf[:, h * LANES:(h + 1) * LANES], 1.0).astype(BF16)

    qm = proj(C_QM, C_GATE)
    gm = gqm_ref[...] * (MEM_HEAD_DIM ** -0.5)
    for h in range(MEM_HEADS):
        qh = qm[:, h * LANES:(h + 1) * LANES]
        qm_ref[0, h] = (_rms(qh, MEM_HEAD_DIM) * gm).astype(BF16)

    for c in range(N_BRANCH):
        gl = proj(C_GATE + c * D_MODEL, C_GATE + (c + 1) * D_MODEL)
        gate_ref[0, :, c * D_MODEL:(c + 1) * D_MODEL] = (0.5 * jnp.tanh(0.5 * gl) + 0.5).astype(BF16)


def _mix_in(x, pending, l, W, tables, tm):
    B, S, _ = x.shape
    cos, sina, sinb = tables
    row = lambda w: pl.BlockSpec((1, tm, w), lambda b, i: (b, i, 0))
    heads = lambda n: pl.BlockSpec((1, n, tm, LANES), lambda b, i: (b, 0, i, 0))
    tab = pl.BlockSpec((tm, LANES), lambda b, i: (i, 0))
    ls = functools.partial(_layer_spec, l)
    head_shape = jax.ShapeDtypeStruct((B, MLA_HEADS, S, LANES), BF16)
    pend_args, pend_specs = _expert_args(*pending, B, S, tm) if pending else ([], [])
    out_specs = [row(POOL_WIDTH), heads(MLA_HEADS), heads(MLA_HEADS), heads(MLA_HEADS),
                 heads(MEM_HEADS), row(N_BRANCH * D_MODEL)]
    out_shape = [jax.ShapeDtypeStruct((B, S, POOL_WIDTH), BF16), head_shape, head_shape, head_shape,
                 jax.ShapeDtypeStruct((B, MEM_HEADS, S, LANES), BF16),
                 jax.ShapeDtypeStruct((B, S, N_BRANCH * D_MODEL), BF16)]
    if pending:
        out_specs.append(row(D_MODEL))
        out_shape.append(jax.ShapeDtypeStruct((B, S, D_MODEL), F32))
    return pl.pallas_call(
        functools.partial(_mix_in_kernel, len(pend_args)),
        grid=(B, S // tm),
        in_specs=[row(D_MODEL)] + pend_specs + [
            ls((1, D_MODEL)), ls((D_MODEL, IN_PACKED)),
            ls((1, Q_LORA)), ls((Q_LORA, MLA_HEADS * LANES)),
            ls((1, KV_LORA)), ls((KV_LORA, MLA_HEADS * LANES)), ls((KV_LORA, MLA_HEADS * LANES)),
            tab, tab, tab,
            ls((1, LANES)), ls((1, LANES)), ls((1, LANES))],
        out_specs=out_specs,
        out_shape=out_shape,
        compiler_params=_params("parallel", "parallel"),
        name="mix_in",
    )(x, *pend_args, W["g_mix"], W["w_in"], W["g_cq"], W["w_uq"], W["g_ckv"], W["w_uk"], W["w_uv"],
      cos, sina, sinb, W["g_qn_mla"], W["g_kn_mla"], W["g_qn_mem"])


def _mem_kv_kernel(mem_ref, gmem_ref, w_ref, gkn_ref, km_ref, vm_ref):
    hn = (_rms(mem_ref[0], D_MODEL) * gmem_ref[...]).astype(BF16)
    kv = _dot(hn, w_ref[...])
    for h in range(MEM_HEADS):
        kh = kv[:, h * LANES:(h + 1) * LANES]
        km_ref[0, h] = (_rms(kh, MEM_HEAD_DIM) * gkn_ref[...]).astype(BF16)
        vm_ref[0, h] = kv[:, MEM_WIDTH + h * LANES:MEM_WIDTH + (h + 1) * LANES].astype(BF16)


def _mem_kv(mem, W):
    B, M, _ = mem.shape
    depth = W["w_mem_kv"].shape[0]
    out = pl.BlockSpec((None, 1, MEM_HEADS, M, LANES), lambda l, b: (l, b, 0, 0, 0))
    shape = jax.ShapeDtypeStruct((depth, B, MEM_HEADS, M, LANES), BF16)
    per_layer = lambda *s: pl.BlockSpec((None,) + s, lambda l, b: (l,) + (0,) * len(s))
    return pl.pallas_call(
        _mem_kv_kernel,
        grid=(depth, B),
        in_specs=[pl.BlockSpec((1, M, D_MODEL), lambda l, b: (b, 0, 0)), per_layer(1, D_MODEL),
                  per_layer(D_MODEL, 2 * MEM_WIDTH), per_layer(1, LANES)],
        out_specs=[out, out],
        out_shape=[shape, shape],
        compiler_params=_params("parallel", "parallel"),
        name="mem_kv",
    )(mem, W["g_mem"], W["w_mem_kv"], W["g_kn_mem"])


def _pool_bands():
    i = np.arange(POOL_ROWS)[:, None]
    r = np.arange(-POOL_HALO, POOL_ROWS + POOL_HALO)[None, :]
    bands = [((r - i >= -(w // 2)) & (r - i < w // 2)).astype(np.float32) for w in POOL_WINDOWS]
    return jnp.asarray(np.stack(bands), BF16)


def _pool_mix(seq_len, row0, has_prev, has_next, cur_ref, prev_ref, next_ref, band_ref, pw_ref, ps_ref):
    tm = cur_ref.shape[1]
    prev = jnp.where(has_prev, prev_ref[0], jnp.zeros_like(prev_ref[0]))
    nxt = jnp.where(has_next, next_ref[0], jnp.zeros_like(next_ref[0]))
    ext = jnp.concatenate([prev, cur_ref[0], nxt], axis=0)
    out = []
    for r in range(0, tm, POOL_ROWS):
        win = ext[r:r + POOL_ROWS + 2 * POOL_HALO]
        t = row0 + r + lax.broadcasted_iota(jnp.int32, (POOL_ROWS, 1), 0)
        groups = []
        for g, w in enumerate(POOL_WINDOWS):
            sl = slice(g * POOL_GROUP_DIM, (g + 1) * POOL_GROUP_DIM)
            count = jnp.minimum(t + w // 2, seq_len) - jnp.maximum(t - w // 2, 0)
            u = win[POOL_HALO:POOL_HALO + POOL_ROWS, sl].astype(F32)
            d = _dot(band_ref[g], win[:, sl]) / count.astype(F32) - u
            groups.append(_dot(d.astype(BF16), pw_ref[g]) * ps_ref[:, sl])
        out.append(jnp.concatenate(groups, axis=-1).astype(BF16))
    return jnp.concatenate(out, axis=0)


def _softmax_pv(s, v):
    m = jnp.max(s, axis=-1, keepdims=True)
    p = jnp.exp(s - m)
    l = jnp.sum(p, axis=-1, keepdims=True)
    return _dot(p.astype(BF16), v) / l


def _mla_attn_kernel(seq_len, tk, q_ref, k_ref, v_ref, *refs):
    if len(refs) == 3:
        slab_ref, o_ref, slab_out_ref = refs
        slab_out_ref[...] = slab_ref[...].astype(BF16)
    else:
        (o_ref,) = refs
    outs = []
    for hh in range(2):
        q = q_ref[0, hh]
        m = acc = None
        for c in range(seq_len // tk):
            ks = slice(c * tk, (c + 1) * tk)
            s = lax.dot_general(q, k_ref[0, hh, ks, :], NT_DIMS, preferred_element_type=F32)
            mc = jnp.max(s, axis=-1, keepdims=True)
            m_new = mc if m is None else jnp.maximum(m, mc)
            pv = _dot(jnp.exp2(s - m_new).astype(BF16), v_ref[0, hh, ks, :])
            acc = pv if acc is None else acc * jnp.exp2(m - m_new) + pv
            m = m_new
        outs.append(acc * (1.0 / acc[:, V_DIM:V_DIM + 1]))
    lane = lax.broadcasted_iota(jnp.int32, outs[0].shape, 1)
    o_ref[0] = jnp.where(lane < V_DIM, outs[0], pltpu.roll(outs[1], V_DIM, 1)).astype(BF16)


def _mla_attn(q, k, v, tq, tk, passenger=None):
    B, H, S, _ = q.shape
    grid = (B, H // 2, S // tq)
    kv = pl.BlockSpec((1, 2, S, LANES), lambda b, p, i: (b, p, 0, 0))
    in_specs = [pl.BlockSpec((1, 2, tq, LANES), lambda b, p, i: (b, p, i, 0)), kv, kv]
    out_specs = [pl.BlockSpec((1, tq, 2 * V_DIM), lambda b, p, i: (b, i, p))]
    out_shape = [jax.ShapeDtypeStruct((B, S, H * V_DIM), BF16)]
    args = [q, k, v]
    if passenger is not None:
        flat = passenger.reshape(-1, passenger.shape[-1])
        steps = grid[0] * grid[1] * grid[2]
        rows = flat.shape[0] // steps
        assert rows * steps == flat.shape[0] and rows % 16 == 0, (flat.shape, steps)
        slab = pl.BlockSpec((rows, flat.shape[1]), lambda b, p, i: ((b * grid[1] + p) * grid[2] + i, 0))
        in_specs.append(slab)
        out_specs.append(slab)
        out_shape.append(jax.ShapeDtypeStruct(flat.shape, BF16))
        args.append(flat)
    outs = pl.pallas_call(
        functools.partial(_mla_attn_kernel, S, tk),
        grid=grid,
        in_specs=in_specs, out_specs=out_specs, out_shape=out_shape,
        compiler_params=_params("parallel", "parallel", "arbitrary"),
        name="mla_attn",
    )(*args)
    if passenger is None:
        return outs[0]
    return outs[0], outs[1].reshape(passenger.shape)


def _merge_kernel(is_moe, seq_len, up_ref, prev_ref, next_ref, band_ref, pw_ref, ps_ref,
                  ym_ref, qm_ref, km_ref, vm_ref, gate_ref, wbr_ref, wout_ref,
                  x_ref, gffn_ref, xo_ref, *hf_refs):
    i = pl.program_id(1)
    y_pool = _pool_mix(seq_len, i * up_ref.shape[1], i > 0, i < pl.num_programs(1) - 1,
                       up_ref, prev_ref, next_ref, band_ref, pw_ref, ps_ref)

    mem = []
    for h in range(MEM_HEADS):
        s = lax.dot_general(qm_ref[0, h], km_ref[0, h], NT_DIMS, preferred_element_type=F32)
        mem.append(_softmax_pv(s, vm_ref[0, h]).astype(BF16))
    y_mem = jnp.concatenate(mem, axis=-1)

    branches = (y_pool, ym_ref[0], y_mem)
    merged = None
    for c, y in enumerate(branches):
        term = gate_ref[0, :, c * D_MODEL:(c + 1) * D_MODEL].astype(F32) * _dot(y, wbr_ref[c])
        merged = term if merged is None else merged + term
    x_new = x_ref[0] + _dot(merged.astype(BF16), wout_ref[...])
    xo_ref[0] = x_new
    hf = _rms(x_new, D_MODEL) * gffn_ref[...]
    if is_moe:
        for c, ref in enumerate(hf_refs):
            ref[0] = hf[:, c * SC_COLS:(c + 1) * SC_COLS]
    else:
        hf_refs[0][0] = hf.astype(BF16)


def _merge(is_moe, up, ym, qm, km, vm, gate, x, l, W, tm):
    B, S, _ = x.shape
    M = km.shape[3]
    row = lambda w: pl.BlockSpec((1, tm, w), lambda b, i: (b, i, 0))
    memkv = pl.BlockSpec((None, 1, MEM_HEADS, M, LANES), lambda b, i: (l, b, 0, 0, 0))
    per = tm // POOL_HALO
    last = S // POOL_HALO - 1
    halo = lambda index: pl.BlockSpec((1, POOL_HALO, POOL_WIDTH), lambda b, i: (b, index(i), 0))
    in_specs = [row(POOL_WIDTH), halo(lambda i: jnp.maximum(i * per - 1, 0)),
                halo(lambda i: jnp.minimum((i + 1) * per, last)),
                _const_spec((N_POOL_GROUPS, POOL_ROWS, POOL_ROWS + 2 * POOL_HALO)),
                _layer_spec(l, (N_POOL_GROUPS, POOL_GROUP_DIM, POOL_GROUP_DIM)), _layer_spec(l, (1, POOL_WIDTH)),
                row(MLA_HEADS * V_DIM),
                pl.BlockSpec((1, MEM_HEADS, tm, LANES), lambda b, i: (b, 0, i, 0)), memkv, memkv,
                row(N_BRANCH * D_MODEL), _layer_spec(l, (N_BRANCH, POOL_WIDTH, D_MODEL)),
                _layer_spec(l, (D_MODEL, D_MODEL)), row(D_MODEL), _layer_spec(l, (1, D_MODEL))]
    out_specs = [row(D_MODEL)]
    out_shape = [jax.ShapeDtypeStruct((B, S, D_MODEL), F32)]
    if is_moe:
        out_specs += [row(SC_COLS)] * SC_CHUNKS
        out_shape += [jax.ShapeDtypeStruct((B, S, SC_COLS), F32)] * SC_CHUNKS
    else:
        out_specs.append(row(D_MODEL))
        out_shape.append(jax.ShapeDtypeStruct((B, S, D_MODEL), BF16))
    return pl.pallas_call(
        functools.partial(_merge_kernel, is_moe, S),
        grid=(B, S // tm),
        in_specs=in_specs, out_specs=out_specs, out_shape=out_shape,
        compiler_params=_params("parallel", "parallel"),
        name="merge_moe" if is_moe else "merge",
    )(up, up, up, _pool_bands(), W["pool_w"], W["pool_scale"],
      ym, qm, km, vm, gate, W["w_br"], W["w_out"], x, W["g_ffn"])


def _swiglu_chunk(h, wg, wu):
    g = _dot(h, wg)
    return g * jax.nn.sigmoid(g) * _dot(h, wu)


def _swiglu_down(h, wgu_ref, wd_ref, acc):
    for j in range(D_FF // FF_CHUNK):
        lo, hi = j * FF_CHUNK, (j + 1) * FF_CHUNK
        a = _swiglu_chunk(h, wgu_ref[:, lo:hi], wgu_ref[:, D_FF + lo:D_FF + hi])
        acc = acc + _dot(a.astype(BF16), wd_ref[lo:hi, :])
    return acc


def _ffn_kernel(hf_ref, wgu_ref, wd_ref, x_ref, o_ref):
    o_ref[...] = _swiglu_down(hf_ref[...], wgu_ref, wd_ref, x_ref[...])


def _ffn(hf, x, l, W, tm):
    N = hf.shape[0]
    row = pl.BlockSpec((tm, D_MODEL), lambda i: (i, 0))
    return pl.pallas_call(
        _ffn_kernel,
        grid=(N // tm,),
        in_specs=[row, _layer_spec(l // 2, (D_MODEL, 2 * D_FF)), _layer_spec(l // 2, (D_FF, D_MODEL)), row],
        out_specs=row,
        out_shape=jax.ShapeDtypeStruct((N, D_MODEL), F32),
        compiler_params=_params("parallel"),
        name="ffn",
    )(hf, W["w_gu_dense"], W["w_down_dense"], x)


def _router_kernel(*refs):
    hf_refs, (wrh_ref, wrl_ref, ridx_ref, rw_ref) = refs[:SC_CHUNKS], refs[SC_CHUNKS:]
    logits = None
    for c, hf_ref in enumerate(hf_refs):
        rows = slice(c * SC_COLS, (c + 1) * SC_COLS)
        hf = hf_ref[...]
        hi = hf.astype(BF16)
        lo = (hf - hi.astype(F32)).astype(BF16)
        part = _dot(hi, wrh_ref[rows, :]) + _dot(lo, wrh_ref[rows, :]) + _dot(hi, wrl_ref[rows, :])
        logits = part if logits is None else logits + part
    lane = lax.broadcasted_iota(jnp.int32, logits.shape, 1)
    lg = jnp.where(lane < N_EXPERTS, logits, -jnp.inf)
    m1 = jnp.max(lg, axis=-1, keepdims=True)
    i1 = jnp.min(jnp.where(lg == m1, lane, LANES), axis=-1, keepdims=True)
    lg2 = jnp.where(lane == i1, -jnp.inf, lg)
    m2 = jnp.max(lg2, axis=-1, keepdims=True)
    i2 = jnp.min(jnp.where(lg2 == m2, lane, LANES), axis=-1, keepdims=True)
    e = jnp.exp(m2 - m1)
    w1 = 1.0 / (1.0 + e)
    ridx_ref[...] = jnp.where(lane == 0, i1, jnp.where(lane == 1, i2, 0))
    rw_ref[...] = jnp.where(lane == 0, w1, jnp.where(lane == 1, e * w1, 0.0))


def _router(hf_chunks, l, W, tr):
    n = hf_chunks[0].shape[0]
    out = pl.BlockSpec((tr, LANES), lambda i: (i, 0))
    return pl.pallas_call(
        _router_kernel,
        grid=(n // tr,),
        in_specs=[pl.BlockSpec((tr, SC_COLS), lambda i: (i, 0))] * SC_CHUNKS
                 + [_layer_spec(l // 2, (D_MODEL, LANES)), _layer_spec(l // 2, (D_MODEL, LANES))],
        out_specs=[out, out],
        out_shape=[jax.ShapeDtypeStruct((n, LANES), jnp.int32), jax.ShapeDtypeStruct((n, LANES), F32)],
        compiler_params=_params("parallel"),
        name="router",
    )(*hf_chunks, W["w_router_hi"], W["w_router_lo"])


def _route_plan(ridx, tmx):
    n = ridx.shape[0]
    e = ridx[:, :2].T.reshape(2 * n)
    onehot = (e[:, None] == jnp.arange(N_EXPERTS, dtype=jnp.int32)[None, :]).astype(jnp.int32)
    csum = jnp.cumsum(onehot, axis=0)
    rank = jnp.sum((csum - onehot) * onehot, axis=-1)
    counts = csum[-1]
    padded = ((counts + tmx - 1) // tmx) * tmx
    ends = jnp.cumsum(padded)
    starts = ends - padded
    dest = jnp.sum(onehot * starts[None, :], axis=-1) + rank
    n_tiles = (2 * n) // tmx + N_EXPERTS
    tile_row0 = jnp.arange(n_tiles, dtype=jnp.int32) * tmx
    tile_e = jnp.minimum(jnp.sum((tile_row0[:, None] >= ends[None, :]).astype(jnp.int32), axis=-1),
                         N_EXPERTS - 1)
    group_end = jnp.take(starts + counts, tile_e)
    tile_valid = jnp.where(tile_row0 < ends[-1], jnp.clip(group_end - tile_row0, 0, tmx), 0)
    return dest.reshape(2, n).astype(jnp.int32), tile_e.astype(jnp.int32), tile_valid.astype(jnp.int32), n_tiles * tmx


def _sc_mesh():
    return plsc.VectorSubcoreMesh(core_axis_name="core", subcore_axis_name="subcore")


def _sc_scatter_rows(chunks, dest, n_out):
    n, width = chunks[0].shape
    per_choice = n // SC_WINDOW
    out_type = [jax.ShapeDtypeStruct((n_out, width), chunks[0].dtype)] * len(chunks)

    @pl.kernel(out_type=out_type, mesh=_sc_mesh(), scratch_types=[])
    def scatter(*refs):
        x_refs, i_hbm, o_refs = refs[:len(chunks)], refs[len(chunks)], refs[len(chunks) + 1:]
        for x_hbm, o_hbm in zip(x_refs, o_refs):
            def body(x_vmem, i_vmem, o_hbm=o_hbm):
                pltpu.sync_copy(x_vmem, o_hbm.at[i_vmem.at[0]])

            pltpu.emit_pipeline(
                body,
                grid=(2 * per_choice,),
                in_specs=[pl.BlockSpec((SC_WINDOW, width), lambda i: (i % per_choice, 0)),
                          pl.BlockSpec((1, SC_WINDOW), lambda i: (0, i))],
                out_specs=[],
                core_axis_name=("core", "subcore"),
                dimension_semantics=(pltpu.PARALLEL,),
            )(x_hbm, i_hbm)

    return scatter(*chunks, dest.reshape(1, 2 * n))


def _sc_gather_rows(chunks, idx):
    n = idx.shape[0]
    width = chunks[0].shape[1]
    out_type = [jax.ShapeDtypeStruct((n, width), chunks[0].dtype)] * len(chunks)

    @pl.kernel(out_type=out_type, mesh=_sc_mesh(), scratch_types=[])
    def gather(*refs):
        x_refs, i_hbm, o_refs = refs[:len(chunks)], refs[len(chunks)], refs[len(chunks) + 1:]
        for x_hbm, o_hbm in zip(x_refs, o_refs):
            def body(i_vmem, o_vmem, x_hbm=x_hbm):
                pltpu.sync_copy(x_hbm.at[i_vmem.at[0]], o_vmem)

            pltpu.emit_pipeline(
                body,
                grid=(n // SC_WINDOW,),
                in_specs=[pl.BlockSpec((1, SC_WINDOW), lambda i: (0, i))],
                out_specs=[pl.BlockSpec((SC_WINDOW, width), lambda i: (i, 0))],
                core_axis_name=("core", "subcore"),
                dimension_semantics=(pltpu.PARALLEL,),
            )(i_hbm, o_hbm)

    return gather(*chunks, idx.reshape(1, n))


def _pack_bf16_pair(lo, hi):
    bits = lambda a: lax.bitcast_convert_type(a.astype(BF16).astype(F32), jnp.uint32)
    return (bits(lo) >> 16) | (bits(hi) & jnp.uint32(0xFFFF0000))


def _unpack_bf16_pair(packed):
    f32 = lambda a: lax.bitcast_convert_type(a, F32)
    return f32(packed << 16), f32(packed & jnp.uint32(0xFFFF0000))


def _expert_sum(x, rw, y_refs):
    w0, w1 = rw[:, 0:1], rw[:, 1:2]
    cols = []
    for c in range(SC_CHUNKS // 2):
        lo0, hi0 = _unpack_bf16_pair(y_refs[2 * c][...])
        lo1, hi1 = _unpack_bf16_pair(y_refs[2 * c + 1][...])
        cols += [w0 * lo0 + w1 * lo1, w0 * hi0 + w1 * hi1]
    return x + jnp.concatenate(cols, axis=-1)


def _moe_group_kernel(te_ref, tv_ref, *refs):
    xs_refs, (wgu_ref, wd_ref) = refs[:SC_CHUNKS], refs[SC_CHUNKS:SC_CHUNKS + 2]
    ys_refs = refs[SC_CHUNKS + 2:]
    valid = tv_ref[pl.program_id(0)]

    @pl.when(valid > 0)
    def _():
        row = lax.broadcasted_iota(jnp.int32, xs_refs[0].shape, 0)
        h = jnp.concatenate([jnp.where(row < valid, xs_ref[...], 0.0).astype(BF16) for xs_ref in xs_refs], axis=-1)
        y = _swiglu_down(h, wgu_ref.at[0], wd_ref.at[0], jnp.zeros((h.shape[0], D_MODEL), F32))
        for c, ys_ref in enumerate(ys_refs):
            lo = y[:, 2 * c * SC_COLS:(2 * c + 1) * SC_COLS]
            hi = y[:, (2 * c + 1) * SC_COLS:(2 * c + 2) * SC_COLS]
            ys_ref[...] = _pack_bf16_pair(lo, hi)


def _moe_group(xs, tile_e, tile_valid, l, W, tmx):
    n_rows = xs[0].shape[0]
    chunk = pl.BlockSpec((tmx, SC_COLS), lambda t, te, tv: (t, 0))
    grid_spec = pltpu.PrefetchScalarGridSpec(
        num_scalar_prefetch=2,
        grid=(n_rows // tmx,),
        in_specs=[chunk] * SC_CHUNKS + [
            pl.BlockSpec((None, 1, D_MODEL, 2 * D_FF), lambda t, te, tv: (l // 2, te[t], 0, 0)),
            pl.BlockSpec((None, 1, D_FF, D_MODEL), lambda t, te, tv: (l // 2, te[t], 0, 0))],
        out_specs=[chunk] * (SC_CHUNKS // 2),
    )
    return pl.pallas_call(
        _moe_group_kernel,
        grid_spec=grid_spec,
        out_shape=[jax.ShapeDtypeStruct((n_rows, SC_COLS), jnp.uint32)] * (SC_CHUNKS // 2),
        compiler_params=_params("arbitrary"),
        name="moe_group",
    )(tile_e, tile_valid, *xs, W["w_gu_moe"], W["w_down_moe"])


def _moe_combine_kernel(x_ref, rw_ref, *refs):
    y_refs, o_ref = refs[:-1], refs[-1]
    o_ref[0] = _expert_sum(x_ref[0], rw_ref[0], [y.at[0, 0] for y in y_refs])


def _expert_args(yg, rw, batch, seq_len, tm):
    args = [rw.reshape(batch, seq_len, LANES)]
    specs = [pl.BlockSpec((1, tm, LANES), lambda b, i: (b, i, 0))]
    for y in yg:
        y4 = y.reshape(2, batch, seq_len, SC_COLS)
        for choice in range(2):
            args.append(y4)
            specs.append(pl.BlockSpec((1, 1, tm, SC_COLS), lambda b, i, choice=choice: (choice, b, i, 0)))
    return args, specs


def _moe_combine(x, yg, rw, tm):
    B, S, _ = x.shape
    row = pl.BlockSpec((1, tm, D_MODEL), lambda b, i: (b, i, 0))
    args, specs = _expert_args(yg, rw, B, S, tm)
    return pl.pallas_call(
        _moe_combine_kernel,
        grid=(B, S // tm),
        in_specs=[row] + specs,
        out_specs=row,
        out_shape=jax.ShapeDtypeStruct((B, S, D_MODEL), F32),
        compiler_params=_params("parallel", "parallel"),
        name="moe_combine",
    )(x, *args)


def _moe(hf_chunks, l, W):
    n = hf_chunks[0].shape[0]
    ridx, rw = _router(hf_chunks, l, W, min(ROUTER_ROWS, n))
    dest, tile_e, tile_valid, n_rows = _route_plan(ridx, MOE_TILE)
    xs = _sc_scatter_rows(hf_chunks, dest, n_rows)
    ys = _moe_group(xs, tile_e, tile_valid, l, W, MOE_TILE)
    yg = _sc_gather_rows(ys, dest.reshape(2 * n))
    return yg, rw


def _rope_tables(seq_len):
    half = QK_ROPE // 2
    freqs = jnp.power(ROPE_THETA, -jnp.arange(half, dtype=F32) / half)
    ang = jnp.arange(seq_len).astype(F32)[:, None] * freqs[None, :]
    c, s = jnp.cos(ang), jnp.sin(ang)
    z = lambda n: jnp.zeros((seq_len, n), F32)
    o = lambda n: jnp.ones((seq_len, n), F32)
    tail = LANES - QK_NOPE - QK_ROPE
    cos = jnp.concatenate([o(QK_NOPE), c, c, o(tail)], axis=-1)
    sina = jnp.concatenate([z(QK_NOPE), -s, z(half), z(tail)], axis=-1)
    sinb = jnp.concatenate([z(QK_NOPE), z(half), s, z(tail)], axis=-1)
    return cos, sina, sinb


def _pad_last(a, width):
    return jnp.pad(a, [(0, 0)] * (a.ndim - 1) + [(0, width - a.shape[-1])])


def _cast_kernel(*refs):
    x_refs, o_ref = refs[:-1], refs[-1]
    width = x_refs[0].shape[1]
    for s, x_ref in enumerate(x_refs):
        o_ref[:, s * width:(s + 1) * width] = x_ref[...].astype(o_ref.dtype)


def _to_bf16(a):
    flat = a.reshape(-1, a.shape[-1])
    rows, cols = flat.shape
    bf16_rows = 16
    cap = CAST_BLOCK_BYTES // (4 * cols)
    tr = max(t for t in range(bf16_rows, cap + 1, bf16_rows) if rows % t == 0)
    width = cols // CAST_STREAMS
    assert width % LANES == 0, (cols, CAST_STREAMS)
    slab = lambda s: pl.BlockSpec((tr, width), lambda i: (i, s))
    out = pl.pallas_call(
        _cast_kernel,
        grid=(rows // tr,),
        in_specs=[slab(s) for s in range(CAST_STREAMS)],
        out_specs=pl.BlockSpec((tr, cols), lambda i: (i, 0)),
        out_shape=jax.ShapeDtypeStruct(flat.shape, BF16),
        compiler_params=_params("parallel"),
        name="cast_bf16",
    )(*([flat] * CAST_STREAMS))
    return out.reshape(a.shape)


def _repack_w_in_kernel(w_ref, o_ref):
    kr_lo = POOL_WIDTH + Q_LORA + KV_LORA
    kr_hi = kr_lo + QK_ROPE
    w = w_ref[...]
    o_ref[:, :C_KR] = w[:, :kr_lo].astype(BF16)
    head = w[:, kr_lo:kr_lo + LANES]
    lane = lax.broadcasted_iota(jnp.int32, head.shape, 1)
    in_slot = (lane >= QK_NOPE) & (lane < QK_NOPE + QK_ROPE)
    o_ref[:, C_KR:C_QM] = jnp.where(in_slot, pltpu.roll(head, QK_NOPE, 1), 0.0).astype(BF16)
    o_ref[:, C_QM:] = w[:, kr_hi:].astype(BF16)


def _repack_w_in(w_in):
    depth, rows, width = w_in.shape
    return pl.pallas_call(
        _repack_w_in_kernel,
        grid=(depth, rows // REPACK_ROWS),
        in_specs=[pl.BlockSpec((None, REPACK_ROWS, width), lambda l, i: (l, i, 0))],
        out_specs=pl.BlockSpec((None, REPACK_ROWS, IN_PACKED), lambda l, i: (l, i, 0)),
        out_shape=jax.ShapeDtypeStruct((depth, rows, IN_PACKED), BF16),
        compiler_params=_params("parallel", "parallel"),
        name="repack_w_in",
    )(w_in)


def _prep_weights(g_mix, w_in, pool_w, pool_scale, g_cq, w_uq, g_ckv, w_ukv, g_qn_mla, g_kn_mla, g_mem,
                  w_mem_kv, g_qn_mem, g_kn_mem, w_br, w_out, g_ffn, w_gu_dense, w_down_dense, w_router):
    depth = w_in.shape[0]
    w_in_p = _repack_w_in(w_in)
    w_uq_p = _pad_last(w_uq.astype(BF16).reshape(depth, Q_LORA, MLA_HEADS, QK_DIM), LANES)
    ukv = w_ukv.astype(BF16).reshape(depth, KV_LORA, MLA_HEADS, QK_NOPE + V_DIM)
    w_uk_p = _pad_last(ukv[..., :QK_NOPE], LANES)
    w_uv_p = _pad_last(ukv[..., QK_NOPE:], LANES)
    row = lambda a: a[:, None, :]
    wr = _pad_last(w_router, LANES)
    wr_hi = wr.astype(BF16)
    return {
        "g_mix": row(g_mix), "w_in": w_in_p,
        "g_cq": row(g_cq), "w_uq": w_uq_p.reshape(depth, Q_LORA, MLA_HEADS * LANES),
        "g_ckv": row(g_ckv), "w_uk": w_uk_p.reshape(depth, KV_LORA, MLA_HEADS * LANES),
        "w_uv": w_uv_p.reshape(depth, KV_LORA, MLA_HEADS * LANES),
        "g_qn_mla": _pad_last(row(g_qn_mla), LANES), "g_kn_mla": _pad_last(row(g_kn_mla), LANES),
        "g_qn_mem": row(g_qn_mem), "g_kn_mem": row(g_kn_mem),
        "g_mem": row(g_mem), "w_mem_kv": w_mem_kv.astype(BF16),
        "pool_w": pool_w.astype(BF16), "pool_scale": row(pool_scale),
        "w_br": w_br.astype(BF16), "w_out": w_out.astype(BF16), "g_ffn": row(g_ffn),
        "w_gu_dense": _to_bf16(w_gu_dense), "w_down_dense": _to_bf16(w_down_dense),
        "w_router_hi": wr_hi, "w_router_lo": (wr - wr_hi.astype(F32)).astype(BF16),
    }


def _layer(x, pending, memkv, l, W, tables, passenger=None):
    B, S, _ = x.shape
    tm = min(MIX_ROWS, S)
    tg = min(MERGE_ROWS, S)
    tq = S if S <= ATTN_WHOLE_SEQ else ATTN_Q_ROWS
    tk = min(max(S // 2, 1024), S)
    tn = min(FFN_ROWS, B * S)
    if pending:
        up, q, k, v, qm, gate, x = _mix_in(x, pending, l, W, tables, tm)
    else:
        up, q, k, v, qm, gate = _mix_in(x, None, l, W, tables, tm)
    km, vm = memkv
    cast = None
    if passenger is None:
        ym = _mla_attn(q, k, v, tq, tk)
    else:
        ym, cast = _mla_attn(q, k, v, tq, tk, passenger)
    if l % 2 == 0:
        x_mid, hf = _merge(False, up, ym, qm, km, vm, gate, x, l, W, tg)
        x = _ffn(hf.reshape(B * S, D_MODEL), x_mid.reshape(B * S, D_MODEL), l, W, tn).reshape(B, S, D_MODEL)
        return x, None, cast
    x, *hf = _merge(True, up, ym, qm, km, vm, gate, x, l, W, tg)
    return x, _moe([h.reshape(B * S, SC_COLS) for h in hf], l, W), cast


def kernel(x_prompt, x_sample, mem_prompt, mem_sample, g_mix, w_in, pool_w, pool_scale, g_cq, w_uq, g_ckv, w_ukv, g_qn_mla, g_kn_mla, g_mem, w_mem_kv, g_qn_mem, g_kn_mem, w_br, w_out, g_ffn, w_gu_dense, w_down_dense, w_router, w_gu_moe, w_down_moe):
    W = _prep_weights(g_mix, w_in, pool_w, pool_scale, g_cq, w_uq, g_ckv, w_ukv, g_qn_mla, g_kn_mla, g_mem,
                      w_mem_kv, g_qn_mem, g_kn_mem, w_br, w_out, g_ffn, w_gu_dense, w_down_dense, w_router)
    xs = [x_prompt, x_sample]
    mems = [_mem_kv(mem_prompt, W), _mem_kv(mem_sample, W)]
    tables = [_rope_tables(x.shape[1]) for x in xs]
    pending = [None, None]
    passengers = [("w_gu_moe", w_gu_moe), ("w_down_moe", w_down_moe)]
    for l in range(DEPTH):
        for t in range(2):
            name, weight = passengers[t] if l == 0 else (None, None)
            xs[t], pending[t], cast = _layer(xs[t], pending[t], mems[t], l, W, tables[t], weight)
            if name:
                W[name] = cast
    for t in range(2):
        if pending[t]:
            xs[t] = _moe_combine(xs[t], *pending[t], min(MIX_ROWS, xs[t].shape[1]))
    return tuple(xs)
```
